```python
import jax, jax.numpy as jnp
from jax import lax
import numpy as np

D_MODEL = 4096
BATCH = 4
SEQ = 2048
DEPTH = 2
DEC_BATCH = 32
DEC_SEQ = 8
PAST_LEN = 16384
PAGE_SIZE = 128

H_A = D_MODEL // 256
DK_A = 128
DV_A = 128
CHUNK = 64
F_MIN = 1e-30
C_B = D_MODEL // 2
CONV_W = 31
HD_C = 64
H_C = D_MODEL // 128
KV_C = H_C // 8
G_C = H_C // KV_C
WINDOW = 128
ROT_DIM = HD_C // 4
ROPE_THETA = 500000.0
NEG_BIG = -1e30
D_FF = 7 * D_MODEL // 2
N_EXPERTS = 8
TOP_K = 2
N_DENSE = (DEPTH + 1) // 2
N_MOE = DEPTH // 2
RMS_EPS = 1e-6
LN_EPS = 1e-5

SPLIT_SIZES = (H_A * DK_A, H_A * DK_A, H_A * DV_A, H_A * DV_A, 2 * C_B,
               H_C * HD_C, KV_C * HD_C, KV_C * HD_C, D_MODEL, D_MODEL, D_MODEL)
D_IN = 2 * H_A * DK_A + 2 * H_A * DV_A + 2 * C_B + (H_C + 2 * KV_C) * HD_C + 3 * D_MODEL

kernel_name = 'hybrid_hgrn2_conformer_swa_sink_decode_step'


def _split_points():
    pts, acc = [], 0
    for s in SPLIT_SIZES[:-1]:
        acc += s
        pts.append(acc)
    return pts


def rmsnorm(x, g):
    xf = x.astype(jnp.float32)
    y = xf * lax.rsqrt(jnp.mean(xf * xf, axis=-1, keepdims=True) + RMS_EPS)
    return (y * g.astype(jnp.float32)).astype(x.dtype)


def layernorm(x, g, b):
    xf = x.astype(jnp.float32)
    mu = jnp.mean(xf, axis=-1, keepdims=True)
    xc = xf - mu
    var = jnp.mean(xc * xc, axis=-1, keepdims=True)
    y = xc * lax.rsqrt(var + LN_EPS) * g.astype(jnp.float32) + b.astype(jnp.float32)
    return y.astype(x.dtype)


def rope_partial(x, pos):
    inv_freq = jnp.float32(ROPE_THETA) ** (-jnp.arange(0, ROT_DIM, 2, dtype=jnp.float32) / ROT_DIM)
    ang = pos.astype(jnp.float32)[:, None] * inv_freq[None, :]
    cos = jnp.cos(ang)[None, :, None, :]
    sin = jnp.sin(ang)[None, :, None, :]
    xf = x.astype(jnp.float32)
    x1 = xf[..., :ROT_DIM // 2]
    x2 = xf[..., ROT_DIM // 2:ROT_DIM]
    out = jnp.concatenate([x1 * cos - x2 * sin, x2 * cos + x1 * sin, xf[..., ROT_DIM:]], axis=-1)
    return out.astype(x.dtype)


def hgrn_lower_bounds(lb_param):
    p = jax.nn.softmax(lb_param.astype(jnp.float32), axis=0)
    return jnp.cumsum(p, axis=0) - p[0:1]


def hgrn2_chunk_scan(q, k, v, log_f, s0):
    B, T = q.shape[:2]
    L = CHUNK if T % CHUNK == 0 else T
    nc = T // L

    def to_chunks(a):
        return a.reshape(B, nc, L, H_A, a.shape[-1]).transpose(1, 0, 3, 2, 4)

    causal = jnp.tril(jnp.ones((L, L), dtype=bool))[None, None, :, :, None]

    def step(S, xs):
        qc, kc, vc, gc = xs
        b = jnp.cumsum(gc, axis=2)
        b_last = b[:, :, -1]
        diff = jnp.where(causal, b[:, :, :, None, :] - b[:, :, None, :, :], 0.0)
        decay = jnp.where(causal, jnp.exp(diff), 0.0)
        scores = jnp.einsum('bhtd,bhsd,bhtsd->bhts', qc, kc, decay)
        o = (jnp.einsum('bhts,bhsv->bhtv', scores, vc)
             + jnp.einsum('bhtd,bhdv->bhtv', qc * jnp.exp(b), S))
        S_new = (jnp.exp(b_last)[..., None] * S
                 + jnp.einsum('bhsd,bhsv->bhdv', kc * jnp.exp(b_last[:, :, None, :] - b), vc))
        return S_new, o

    S_fin, o = lax.scan(step, s0, (to_chunks(q), to_chunks(k), to_chunks(v), to_chunks(log_f)))
    o = o.transpose(1, 0, 3, 2, 4).reshape(B, T, H_A, DV_A)
    return o, S_fin


def window_attention(q, k, v, q_pos, k_pos, sinks):
    logits = jnp.einsum('bnqhgd,bnshd->bnhgqs', q.astype(jnp.float32), k.astype(jnp.float32)) * (HD_C ** -0.5)
    dist = q_pos[:, :, None] - k_pos[:, None, :]
    valid = (dist >= 0) & (dist < WINDOW) & (k_pos[:, None, :] >= 0)
    valid = valid[None, :, None, None]
    logits = jnp.where(valid, logits, NEG_BIG)
    sink = sinks.astype(jnp.float32).reshape(KV_C, G_C)[None, None, :, :, None, None]
    m = jnp.maximum(jnp.max(logits, axis=-1, keepdims=True), sink)
    p = jnp.where(valid, jnp.exp(logits - m), 0.0)
    denom = jnp.sum(p, axis=-1, keepdims=True) + jnp.exp(sink - m)
    o = jnp.einsum('bnhgqs,bnshd->bnqhgd', p / denom, v.astype(jnp.float32))
    return o.astype(q.dtype)


def banded_window_attention(q, k, v, sinks):
    B, T = q.shape[:2]
    nb = T // WINDOW
    qb = q.reshape(B, nb, WINDOW, KV_C, G_C, HD_C)

    def with_prev(a):
        ab = a.reshape(B, nb, WINDOW, KV_C, HD_C)
        prev = jnp.concatenate([jnp.zeros_like(ab[:, :1]), ab[:, :-1]], axis=1)
        return jnp.concatenate([prev, ab], axis=2)

    q_pos = jnp.arange(T).reshape(nb, WINDOW)
    k_pos = (jnp.arange(nb)[:, None] - 1) * WINDOW + jnp.arange(2 * WINDOW)[None, :]
    o = window_attention(qb, with_prev(k), with_prev(v), q_pos, k_pos, sinks)
    return o.reshape(B, T, H_C, HD_C)


def swiglu(h, w1, w3, w2):
    return (jax.nn.silu(h @ w1) * (h @ w3)) @ w2


def moe_swiglu(h, w_router, w1, w3, w2):
    B, T, D = h.shape
    t = h.reshape(B * T, D)
    logits = (t @ w_router).astype(jnp.float32)
    top_v, top_i = lax.top_k(logits, TOP_K)
    w = jax.nn.softmax(top_v, axis=-1)
    gates = jnp.sum(jax.nn.one_hot(top_i, N_EXPERTS, dtype=jnp.float32) * w[..., None], axis=1)
    out = jnp.zeros((B * T, D), jnp.float32)
    for e in range(N_EXPERTS):
        out = out + gates[:, e:e + 1] * swiglu(t, w1[e], w3[e], w2[e]).astype(jnp.float32)
    return out.astype(h.dtype).reshape(B, T, D)


def layer_forward(x, l, start, s_hgrn, conv_buf, k_buf, v_buf, P):
    f32 = jnp.float32
    B, T, _ = x.shape
    h = rmsnorm(x, P['g_mix'][l])
    proj = h @ P['w_in'][l]
    qa, fa, ia, ga, ub, qc, kc, vc, gt_a, gt_b, gt_c = jnp.split(proj, _split_points(), axis=-1)

    lb = hgrn_lower_bounds(P['lb_param'])[l].reshape(H_A, DK_A)
    z = fa.astype(f32).reshape(B, T, H_A, DK_A)
    sig = jax.nn.sigmoid(z)
    f_gate = lb + (1.0 - lb) * sig
    log_f = jnp.log(jnp.maximum(f_gate, F_MIN))
    k_a = (1.0 - lb) * (1.0 - sig)
    q_a = jax.nn.silu(qa.astype(f32)).reshape(B, T, H_A, DK_A)
    v_a = ia.astype(f32).reshape(B, T, H_A, DV_A)
    o_a, s_new = hgrn2_chunk_scan(q_a, k_a, v_a, log_f, s_hgrn.astype(f32))
    o_a = rmsnorm(o_a, P['g_hgrn'][l]) * jax.nn.silu(ga.astype(f32).reshape(B, T, H_A, DV_A))
    y_a = o_a.reshape(B, T, H_A * DV_A).astype(x.dtype) @ P['w_out_a'][l]

    u_lin, u_gate = jnp.split(ub, 2, axis=-1)
    u = u_lin * jax.nn.sigmoid(u_gate)
    padded = jnp.concatenate([conv_buf.astype(u.dtype), u], axis=1)
    conv = lax.conv_general_dilated(padded, P['w_dw'][l][:, None, :].astype(u.dtype),
                                    window_strides=(1,), padding='VALID',
                                    dimension_numbers=('NWC', 'WIO', 'NWC'),
                                    feature_group_count=C_B) + P['b_dw'][l]
    y_b = jax.nn.silu(layernorm(conv, P['ln_g'][l], P['ln_b'][l])) @ P['w_out_b'][l]
    new_conv = padded[:, -(CONV_W - 1):]

    pos = start + jnp.arange(T)
    q_c = rope_partial(rmsnorm(qc.reshape(B, T, H_C, HD_C), P['g_qn'][l]), pos)
    k_c = rope_partial(rmsnorm(kc.reshape(B, T, KV_C, HD_C), P['g_kn'][l]), pos)
    v_c = vc.reshape(B, T, KV_C, HD_C)
    sinks = P['sinks'][l]
    if k_buf is None:
        o_c = banded_window_attention(q_c, k_c, v_c, sinks)
        new_k, new_v = k_c[:, -WINDOW:], v_c[:, -WINDOW:]
    else:
        keys = jnp.concatenate([k_buf.astype(k_c.dtype), k_c], axis=1)
        vals = jnp.concatenate([v_buf.astype(v_c.dtype), v_c], axis=1)
        q_pos = pos[None]
        k_pos = (start - WINDOW + jnp.arange(WINDOW + T))[None]
        o_c = window_attention(q_c.reshape(B, 1, T, KV_C, G_C, HD_C), keys[:, None], vals[:, None],
                               q_pos, k_pos, sinks).reshape(B, T, H_C, HD_C)
        new_k, new_v = keys[:, -WINDOW:], vals[:, -WINDOW:]
    y_c = o_c.reshape(B, T, H_C * HD_C) @ P['w_out_c'][l]

    mixed = jax.nn.sigmoid(gt_a) * y_a + jax.nn.sigmoid(gt_b) * y_b + jax.nn.sigmoid(gt_c) * y_c
    x = x + mixed @ P['w_o'][l]

    h2 = rmsnorm(x, P['g_ffn'][l])
    j = l // 2
    if l % 2 == 0:
        ffn = swiglu(h2, P['w1_dense'][j], P['w3_dense'][j], P['w2_dense'][j])
    else:
        ffn = moe_swiglu(h2, P['w_router'][j], P['w1_moe'][j], P['w3_moe'][j], P['w2_moe'][j])
    x = x + ffn
    return x, s_new.astype(x.dtype), new_conv.astype(x.dtype), new_k, new_v


def setup_inputs(seed: int = 0) -> dict:
    key = jax.random.key(seed)
    ks = jax.random.split(key, 32)
    f32 = jnp.float32

    def nrm(k, shape, scale):
        return jax.random.normal(k, shape, f32) * scale

    def gain(k, shape):
        return 1.0 + 0.02 * jax.random.normal(k, shape, f32)

    return {
        'x_prompt': nrm(ks[0], (BATCH, SEQ, D_MODEL), 1.0),
        'x_sample': nrm(ks[1], (DEC_BATCH, DEC_SEQ, D_MODEL), 1.0),
        'state_hgrn': nrm(ks[2], (DEPTH, DEC_BATCH, H_A, DK_A, DV_A), 0.5),
        'state_conv': nrm(ks[3], (DEPTH, DEC_BATCH, CONV_W - 1, C_B), 0.5),
        'cache_k': nrm(ks[4], (DEPTH, DEC_BATCH, WINDOW, KV_C, HD_C), 1.0),
        'cache_v': nrm(ks[5], (DEPTH, DEC_BATCH, WINDOW, KV_C, HD_C), 1.0),
        'lb_param': nrm(ks[6], (DEPTH, H_A * DK_A), 0.5),
        'g_mix': gain(ks[7], (DEPTH, D_MODEL)),
        'w_in': nrm(ks[8], (DEPTH, D_MODEL, D_IN), D_MODEL ** -0.5),
        'g_hgrn': gain(ks[9], (DEPTH, DV_A)),
        'w_out_a': nrm(ks[10], (DEPTH, H_A * DV_A, D_MODEL), (H_A * DV_A) ** -0.5),
        'w_dw': nrm(ks[11], (DEPTH, CONV_W, C_B), CONV_W ** -0.5),
        'b_dw': nrm(ks[12], (DEPTH, C_B), 0.02),
        'ln_g': gain(ks[13], (DEPTH, C_B)),
        'ln_b': nrm(ks[14], (DEPTH, C_B), 0.02),
        'w_out_b': nrm(ks[15], (DEPTH, C_B, D_MODEL), C_B ** -0.5),
        'g_qn': gain(ks[16], (DEPTH, HD_C)),
        'g_kn': gain(ks[17], (DEPTH, HD_C)),
        'sinks': nrm(ks[18], (DEPTH, H_C), 1.0),
        'w_out_c': nrm(ks[19], (DEPTH, H_C * HD_C, D_MODEL), (H_C * HD_C) ** -0.5),
        'w_o': nrm(ks[20], (DEPTH, D_MODEL, D_MODEL), D_MODEL ** -0.5),
        'g_ffn': gain(ks[21], (DEPTH, D_MODEL)),
        'w1_dense': nrm(ks[22], (N_DENSE, D_MODEL, D_FF), D_MODEL ** -0.5),
        'w3_dense': nrm(ks[23], (N_DENSE, D_MODEL, D_FF), D_MODEL ** -0.5),
        'w2_dense': nrm(ks[24], (N_DENSE, D_FF, D_MODEL), D_FF ** -0.5),
        'w_router': nrm(ks[25], (N_MOE, D_MODEL, N_EXPERTS), D_MODEL ** -0.5),
        'w1_moe': nrm(ks[26], (N_MOE, N_EXPERTS, D_MODEL, D_FF), D_MODEL ** -0.5),
        'w3_moe': nrm(ks[27], (N_MOE, N_EXPERTS, D_MODEL, D_FF), D_MODEL ** -0.5),
        'w2_moe': nrm(ks[28], (N_MOE, N_EXPERTS, D_FF, D_MODEL), D_FF ** -0.5),
    }


def reference(x_prompt, x_sample, state_hgrn, state_conv, cache_k, cache_v,
              lb_param, g_mix, w_in, g_hgrn, w_out_a, w_dw, b_dw, ln_g, ln_b, w_out_b,
              g_qn, g_kn, sinks, w_out_c, w_o, g_ffn, w1_dense, w3_dense, w2_dense,
              w_router, w1_moe, w3_moe, w2_moe):
    P = dict(lb_param=lb_param, g_mix=g_mix, w_in=w_in, g_hgrn=g_hgrn, w_out_a=w_out_a,
             w_dw=w_dw, b_dw=b_dw, ln_g=ln_g, ln_b=ln_b, w_out_b=w_out_b, g_qn=g_qn, g_kn=g_kn,
             sinks=sinks, w_out_c=w_out_c, w_o=w_o, g_ffn=g_ffn, w1_dense=w1_dense,
             w3_dense=w3_dense, w2_dense=w2_dense, w_router=w_router, w1_moe=w1_moe,
             w3_moe=w3_moe, w2_moe=w2_moe)
    Bp = x_prompt.shape[0]
    s0_prompt = jnp.zeros((Bp, H_A, DK_A, DV_A), jnp.float32)
    conv0_prompt = jnp.zeros((Bp, CONV_W - 1, C_B), x_prompt.dtype)
    yp, ys = x_prompt, x_sample
    hp, cp, kp, vp, hs, cs, kss, vss = [], [], [], [], [], [], [], []
    for l in range(DEPTH):
        yp, s_p, c_p, k_p, v_p = layer_forward(yp, l, 0, s0_prompt, conv0_prompt, None, None, P)
        ys, s_s, c_s, k_s, v_s = layer_forward(ys, l, PAST_LEN, state_hgrn[l], state_conv[l],
                                               cache_k[l], cache_v[l], P)
        hp.append(s_p); cp.append(c_p); kp.append(k_p); vp.append(v_p)
        hs.append(s_s); cs.append(c_s); kss.append(k_s); vss.append(v_s)
    state_hgrn_prompt = jnp.stack(hp)
    state_conv_prompt = jnp.stack(cp)
    cache_k_prompt = jnp.stack(kp)
    cache_v_prompt = jnp.stack(vp)
    state_hgrn_sample = jnp.stack(hs)
    state_conv_sample = jnp.stack(cs)
    cache_k_sample = jnp.stack(kss)
    cache_v_sample = jnp.stack(vss)
    return (yp, ys, state_hgrn_prompt, state_conv_prompt, cache_k_prompt, cache_v_prompt,
            state_hgrn_sample, state_conv_sample, cache_k_sample, cache_v_sample)
```

```python
import functools

import jax
import jax.numpy as jnp
from jax import lax
from jax.experimental import pallas as pl
from jax.experimental.pallas import tpu as pltpu

F32 = jnp.float32
BF16 = jnp.bfloat16

RMS_EPS = 1e-6
LN_EPS = 1e-5
F_MIN = 1e-30
NEG_BIG = -1e30
ROPE_THETA = 500000.0
HEAD_A = 128
HEAD_C = 64
WINDOW = 128
PAST_LEN = 16384
TOP_K = 2
HIGHEST = lax.Precision.HIGHEST

V7X_VMEM_LIMIT_BYTES = 56 * 1024 * 1024
LANES = 128
SUBLANES = 8


def _pick(n, candidates):
    for c in candidates:
        if n % c == 0:
            return c
    raise ValueError(f"no tile in {candidates} divides {n}")


def _params(*semantics):
    return pltpu.CompilerParams(dimension_semantics=semantics, vmem_limit_bytes=V7X_VMEM_LIMIT_BYTES)


def _silu(x):
    return x * jax.nn.sigmoid(x)


def _rmsnorm_kernel(x_ref, g_ref, o_ref):
    x = x_ref[...]
    ms = jnp.mean(x * x, axis=-1, keepdims=True)
    o_ref[...] = (x * lax.rsqrt(ms + RMS_EPS) * g_ref[...]).astype(o_ref.dtype)


def rmsnorm_rows(x, g, out_dtype):
    m, d = x.shape
    tm = _pick(m, (256, 128, 64, 32, 16))
    return pl.pallas_call(
        _rmsnorm_kernel,
        grid=(m // tm,),
        in_specs=[pl.BlockSpec((tm, d), lambda i: (i, 0)), pl.BlockSpec((1, d), lambda i: (0, 0))],
        out_specs=pl.BlockSpec((tm, d), lambda i: (i, 0)),
        out_shape=jax.ShapeDtypeStruct((m, d), out_dtype),
        compiler_params=_params("parallel"),
        name="rmsnorm_rows",
    )(x, g.reshape(1, d))


def _ln_silu_kernel(x_ref, g_ref, b_ref, o_ref):
    x = x_ref[...]
    mu = jnp.mean(x, axis=-1, keepdims=True)
    xc = x - mu
    var = jnp.mean(xc * xc, axis=-1, keepdims=True)
    y = xc * lax.rsqrt(var + LN_EPS) * g_ref[...] + b_ref[...]
    o_ref[...] = _silu(y).astype(o_ref.dtype)


def layernorm_silu_rows(x, g, b, out_dtype):
    m, d = x.shape
    tm = _pick(m, (256, 128, 64, 32, 16))
    return pl.pallas_call(
        _ln_silu_kernel,
        grid=(m // tm,),
        in_specs=[pl.BlockSpec((tm, d), lambda i: (i, 0)),
                  pl.BlockSpec((1, d), lambda i: (0, 0)),
                  pl.BlockSpec((1, d), lambda i: (0, 0))],
        out_specs=pl.BlockSpec((tm, d), lambda i: (i, 0)),
        out_shape=jax.ShapeDtypeStruct((m, d), out_dtype),
        compiler_params=_params("parallel"),
        name="layernorm_silu_rows",
    )(x, g.reshape(1, d), b.reshape(1, d))


def _gemm_kernel(x_ref, w_ref, o_ref):
    o_ref[...] = jnp.dot(x_ref[...].astype(BF16), w_ref[...].astype(BF16), preferred_element_type=F32)


def _gemm_residual_kernel(x_ref, w_ref, r_ref, o_ref):
    o_ref[...] = r_ref[...] + jnp.dot(x_ref[...].astype(BF16), w_ref[...].astype(BF16),
                                      preferred_element_type=F32)


def gemm(x, w, layer, residual=None):
    m, k = x.shape
    n = w.shape[-1]
    tm = _pick(m, (1056, 1024, 512, 256, 128, 64, 32, 16))
    tn = _pick(n, (512, 256, 128))
    in_specs = [pl.BlockSpec((tm, k), lambda i, j: (i, 0)),
                pl.BlockSpec((None, k, tn), lambda i, j: (layer, 0, j))]
    args = [x, w]
    body = _gemm_kernel
    if residual is not None:
        in_specs.append(pl.BlockSpec((tm, tn), lambda i, j: (i, j)))
        args.append(residual)
        body = _gemm_residual_kernel
    return pl.pallas_call(
        body,
        grid=(m // tm, n // tn),
        in_specs=in_specs,
        out_specs=pl.BlockSpec((tm, tn), lambda i, j: (i, j)),
        out_shape=jax.ShapeDtypeStruct((m, n), F32),
        compiler_params=_params("parallel", "arbitrary"),
        name="gemm_residual" if residual is not None else "gemm",
    )(*args)


def _merge_kernel(oa_ref, ob_ref, oc_ref, wa_ref, wb_ref, wc_ref, ga_ref, gb_ref, gc_ref, o_ref):
    def branch(o_ref_, w_ref_, g_ref_):
        y = jnp.dot(o_ref_[...].astype(BF16), w_ref_[...].astype(BF16), preferred_element_type=F32)
        return jax.nn.sigmoid(g_ref_[...]) * y

    mixed = branch(oa_ref, wa_ref, ga_ref) + branch(ob_ref, wb_ref, gb_ref) + branch(oc_ref, wc_ref, gc_ref)
    o_ref[...] = mixed.astype(o_ref.dtype)


def gated_merge(o_a, o_b, o_c, w_out_a, w_out_b, w_out_c, proj, layer, gate_col, d_model):
    m = o_a.shape[0]
    tm = _pick(m, (528, 512, 256, 128, 64, 32, 16))
    tn = next(c for c in (256, 128) if d_model % c == 0 and gate_col % c == 0)
    gate_blk = gate_col // tn
    per_gate = d_model // tn

    def o_spec(o):
        return pl.BlockSpec((tm, o.shape[1]), lambda i, j: (i, 0))

    def w_spec(w):
        return pl.BlockSpec((None, w.shape[1], tn), lambda i, j: (layer, 0, j))

    def g_spec(which):
        return pl.BlockSpec((tm, tn), lambda i, j: (i, gate_blk + which * per_gate + j))

    return pl.pallas_call(
        _merge_kernel,
        grid=(m // tm, d_model // tn),
        in_specs=[o_spec(o_a), o_spec(o_b), o_spec(o_c), w_spec(w_out_a), w_spec(w_out_b), w_spec(w_out_c),
                  g_spec(0), g_spec(1), g_spec(2)],
        out_specs=pl.BlockSpec((tm, tn), lambda i, j: (i, j)),
        out_shape=jax.ShapeDtypeStruct((m, d_model), BF16),
        compiler_params=_params("parallel", "arbitrary"),
        name="gated_merge",
    )(o_a, o_b, o_c, w_out_a, w_out_b, w_out_c, proj, proj, proj)


def _cumsum_rows(x):
    n = x.shape[0]
    row = lax.broadcasted_iota(jnp.int32, x.shape, 0)
    shift = 1
    while shift < n:
        x = x + jnp.where(row >= shift, pltpu.roll(x, shift, axis=0), 0.0)
        shift *= 2
    return x


def _hgrn_kernel(q_ref, f_ref, i_ref, g_ref, lb_ref, gh_ref, s0_ref, o_ref, sfin_ref,
                 st_ref, k_scr, b_scr, v_scr, o_scr, *, chunk, n_chunks):
    c = pl.program_id(2)

    @pl.when(c == 0)
    def _():
        st_ref[...] = s0_ref[...].T

    lb = lb_ref[...]
    sig = jax.nn.sigmoid(f_ref[...])
    f_gate = lb + (1.0 - lb) * sig
    log_f = jnp.log(jnp.maximum(f_gate, F_MIN))
    k = (1.0 - lb) * (1.0 - sig)
    q = _silu(q_ref[...])
    v = i_ref[...]
    b = _cumsum_rows(log_f)
    k_scr[...] = k
    b_scr[...] = b
    v_scr[...] = v

    st = st_ref[...]
    o_scr[...] = lax.dot_general(q * jnp.exp(b), st, (((1,), (1,)), ((), ())),
                                 precision=HIGHEST, preferred_element_type=F32)

    for r0 in range(0, chunk, SUBLANES):
        rows = r0 + lax.broadcasted_iota(jnp.int32, (chunk - r0, 1), 0)
        q_t = q[r0:, :]
        b_t = b[r0:, :]

        def pair(s8, acc, r0=r0, rows=rows, q_t=q_t, b_t=b_t):
            s = r0 + s8
            k_s = k_scr[pl.ds(s, 1), :]
            b_s = b_scr[pl.ds(s, 1), :]
            v_s = v_scr[pl.ds(s, 1), :]
            causal = rows >= s
            decay = jnp.where(causal, jnp.exp(jnp.where(causal, b_t - b_s, 0.0)), 0.0)
            score = jnp.sum(q_t * k_s * decay, axis=-1, keepdims=True)
            return acc + score * v_s

        o_scr[r0:, :] += lax.fori_loop(0, SUBLANES, pair, jnp.zeros((chunk - r0, HEAD_A), F32), unroll=True)

    o = o_scr[...]
    o_n = o * lax.rsqrt(jnp.mean(o * o, axis=-1, keepdims=True) + RMS_EPS) * gh_ref[...]
    o_ref[...] = o_n * _silu(g_ref[...])

    b_last = b[chunk - 1:chunk, :]
    k_dec = k * jnp.exp(b_last - b)
    st_new = jnp.exp(b_last) * st + lax.dot_general(v, k_dec, (((0,), (0,)), ((), ())),
                                                    precision=HIGHEST, preferred_element_type=F32)
    st_ref[...] = st_new

    @pl.when(c == n_chunks - 1)
    def _():
        sfin_ref[...] = st_new.T


def hgrn_mixer(proj, lb, g_hgrn, s0, *, batch, seq, chunk, row0, heads):
    n_chunks = seq // chunk
    blk0 = row0 // chunk

    def p_spec(slab):
        return pl.BlockSpec((chunk, HEAD_A), lambda b, h, c: (blk0 + b * n_chunks + c, slab * heads + h))

    kern = functools.partial(_hgrn_kernel, chunk=chunk, n_chunks=n_chunks)
    return pl.pallas_call(
        kern,
        grid=(batch, heads, n_chunks),
        in_specs=[p_spec(0), p_spec(1), p_spec(2), p_spec(3),
                  pl.BlockSpec((None, 1, HEAD_A), lambda b, h, c: (h, 0, 0)),
                  pl.BlockSpec((1, HEAD_A), lambda b, h, c: (0, 0)),
                  pl.BlockSpec((None, None, HEAD_A, HEAD_A), lambda b, h, c: (b, h, 0, 0))],
        out_specs=[pl.BlockSpec((chunk, HEAD_A), lambda b, h, c: (b * n_chunks + c, h)),
                   pl.BlockSpec((None, None, HEAD_A, HEAD_A), lambda b, h, c: (b, h, 0, 0))],
        out_shape=[jax.ShapeDtypeStruct((batch * seq, heads * HEAD_A), F32),
                   jax.ShapeDtypeStruct((batch, heads, HEAD_A, HEAD_A), F32)],
        scratch_shapes=[pltpu.VMEM((HEAD_A, HEAD_A), F32)] + [pltpu.VMEM((chunk, HEAD_A), F32)] * 4,
        compiler_params=_params("parallel", "parallel", "arbitrary"),
        name="hgrn_mixer",
    )(proj, proj, proj, proj, lb.reshape(heads, 1, HEAD_A), g_hgrn.reshape(1, HEAD_A), s0)


HIST_ROWS = 32


def _conv_kernel(ul_ref, ug_ref, hist_ref, w_ref, bias_ref, o_ref, state_ref, buf_ref, *, tb, conv_w):
    t = pl.program_id(2)

    @pl.when(t == 0)
    def _():
        buf_ref[0:HIST_ROWS, :] = hist_ref[...]

    u = ul_ref[...] * jax.nn.sigmoid(ug_ref[...])
    buf_ref[HIST_ROWS:HIST_ROWS + tb, :] = u
    first = HIST_ROWS - (conv_w - 1)
    acc = jnp.zeros(u.shape, F32)
    for j in range(conv_w):
        acc = acc + w_ref[j:j + 1, :] * buf_ref[first + j:first + j + tb, :]
    o_ref[...] = acc + bias_ref[...]
    state_ref[...] = buf_ref[tb + first:tb + HIST_ROWS, :]
    tail = buf_ref[tb:tb + HIST_ROWS, :]
    buf_ref[0:HIST_ROWS, :] = tail


def conv_mixer(proj, hist, w_dw, b_dw, *, batch, seq, tb, row0, lin_col, gate_col):
    conv_w, channels = w_dw.shape
    cw = _pick(channels, (256, 128))
    n_t = seq // tb
    blk0 = row0 // tb
    kern = functools.partial(_conv_kernel, tb=tb, conv_w=conv_w)
    return pl.pallas_call(
        kern,
        grid=(batch, channels // cw, n_t),
        in_specs=[pl.BlockSpec((tb, cw), lambda b, c, t: (blk0 + b * n_t + t, lin_col // cw + c)),
                  pl.BlockSpec((tb, cw), lambda b, c, t: (blk0 + b * n_t + t, gate_col // cw + c)),
                  pl.BlockSpec((None, HIST_ROWS, cw), lambda b, c, t: (b, 0, c)),
                  pl.BlockSpec((conv_w, cw), lambda b, c, t: (0, c)),
                  pl.BlockSpec((1, cw), lambda b, c, t: (0, c))],
        out_specs=[pl.BlockSpec((tb, cw), lambda b, c, t: (b * n_t + t, c)),
                   pl.BlockSpec((None, conv_w - 1, cw), lambda b, c, t: (b, 0, c))],
        out_shape=[jax.ShapeDtypeStruct((batch * seq, channels), F32),
                   jax.ShapeDtypeStruct((batch, conv_w - 1, channels), F32)],
        scratch_shapes=[pltpu.VMEM((HIST_ROWS + tb, cw), F32)],
        compiler_params=_params("parallel", "parallel", "arbitrary"),
        name="conv_mixer",
    )(proj, proj, hist, w_dw, b_dw.reshape(1, channels))


def _rope_tables(pos):
    rot = HEAD_C // 4
    inv_freq = jnp.float32(ROPE_THETA) ** (-jnp.arange(0, rot, 2, dtype=F32) / rot)
    ang = pos.astype(F32)[:, None] * inv_freq[None, :]
    cos, sin = jnp.cos(ang), jnp.sin(ang)
    rest = HEAD_C - rot
    c_tab = jnp.concatenate([cos, cos, jnp.ones((pos.shape[0], rest), F32)], axis=-1)
    s_tab = jnp.concatenate([-sin, sin, jnp.zeros((pos.shape[0], rest), F32)], axis=-1)
    return c_tab, s_tab


def _norm_rope(x, gain, c_tab, s_tab):
    half = HEAD_C // 8
    y = x * lax.rsqrt(jnp.mean(x * x, axis=-1, keepdims=True) + RMS_EPS) * gain
    swapped = jnp.concatenate([y[:, half:2 * half], y[:, 0:half], y[:, 2 * half:]], axis=-1)
    return y * c_tab + swapped * s_tab


def _softmax_sink(logits, valid, sink):
    logits = jnp.where(valid, logits, NEG_BIG)
    m = jnp.maximum(jnp.max(logits, axis=-1, keepdims=True), sink)
    p = jnp.where(valid, jnp.exp(logits - m), 0.0)
    denom = jnp.sum(p, axis=-1, keepdims=True) + jnp.exp(sink - m)
    return p / denom


def _qk_logits(q, k):
    return lax.dot_general(q.astype(BF16), k.astype(BF16), (((1,), (1,)), ((), ())),
                           preferred_element_type=F32) * (HEAD_C ** -0.5)


def _pv(p, v):
    return jnp.dot(p.astype(BF16), v.astype(BF16), preferred_element_type=F32)


def _attn_prompt_kernel(sinks_ref, q_ref, kv_ref, c_ref, s_ref, gq_ref, gk_ref, o_ref, newk_ref, newv_ref,
                        kprev_ref, vprev_ref, *, kv_heads, group):
    j = pl.program_id(1)
    kvw = kv_heads * HEAD_C

    @pl.when(j == 0)
    def _():
        kprev_ref[...] = jnp.zeros_like(kprev_ref)
        vprev_ref[...] = jnp.zeros_like(vprev_ref)

    c_tab = c_ref[...]
    s_tab = s_ref[...]
    rows = group * WINDOW
    row = lax.broadcasted_iota(jnp.int32, (rows, 2 * WINDOW), 0)
    cidx = lax.broadcasted_iota(jnp.int32, (rows, 2 * WINDOW), 1)
    r = lax.rem(row, WINDOW)
    first_key = WINDOW - jnp.minimum(j, 1) * WINDOW
    valid = (cidx > r) & (cidx <= r + WINDOW) & (cidx >= first_key)
    head_in_group = lax.broadcasted_iota(jnp.int32, (rows, 1), 0) // WINDOW

    for g in range(kv_heads):
        k_new = _norm_rope(kv_ref[:, g * HEAD_C:(g + 1) * HEAD_C], gk_ref[...], c_tab, s_tab)
        v_new = kv_ref[:, kvw + g * HEAD_C:kvw + (g + 1) * HEAD_C]
        keys = jnp.concatenate([kprev_ref[:, g * HEAD_C:(g + 1) * HEAD_C], k_new], axis=0)
        vals = jnp.concatenate([vprev_ref[:, g * HEAD_C:(g + 1) * HEAD_C], v_new], axis=0)
        heads = [g * group + i for i in range(group)]
        q_g = jnp.concatenate([_norm_rope(q_ref[:, h * HEAD_C:(h + 1) * HEAD_C], gq_ref[...], c_tab, s_tab)
                               for h in heads], axis=0)
        sink = jnp.zeros((rows, 1), F32)
        for i, h in enumerate(heads):
            sink = jnp.where(head_in_group == i, sinks_ref[h], sink)
        o_g = _pv(_softmax_sink(_qk_logits(q_g, keys), valid, sink), vals)
        for i, h in enumerate(heads):
            o_ref[:, h * HEAD_C:(h + 1) * HEAD_C] = o_g[i * WINDOW:(i + 1) * WINDOW, :]
        kprev_ref[:, g * HEAD_C:(g + 1) * HEAD_C] = k_new
        vprev_ref[:, g * HEAD_C:(g + 1) * HEAD_C] = v_new
    newk_ref[...] = kprev_ref[...]
    newv_ref[...] = vprev_ref[...]


def attn_prompt(proj, sinks, g_qn, g_kn, *, batch, seq, q_col, k_col, q_heads, kv_heads):
    nb = seq // WINDOW
    qw = q_heads * HEAD_C
    kvw = kv_heads * HEAD_C
    c_tab, s_tab = _rope_tables(jnp.arange(seq))
    kern = functools.partial(_attn_prompt_kernel, kv_heads=kv_heads, group=q_heads // kv_heads)
    return pl.pallas_call(
        kern,
        grid=(batch, nb),
        in_specs=[pl.BlockSpec(memory_space=pltpu.SMEM),
                  pl.BlockSpec((WINDOW, qw), lambda b, j: (b * nb + j, q_col // qw)),
                  pl.BlockSpec((WINDOW, 2 * kvw), lambda b, j: (b * nb + j, k_col // (2 * kvw))),
                  pl.BlockSpec((WINDOW, HEAD_C), lambda b, j: (j, 0)),
                  pl.BlockSpec((WINDOW, HEAD_C), lambda b, j: (j, 0)),
                  pl.BlockSpec((1, HEAD_C), lambda b, j: (0, 0)),
                  pl.BlockSpec((1, HEAD_C), lambda b, j: (0, 0))],
        out_specs=[pl.BlockSpec((WINDOW, qw), lambda b, j: (b * nb + j, 0)),
                   pl.BlockSpec((None, WINDOW, kvw), lambda b, j: (b, 0, 0)),
                   pl.BlockSpec((None, WINDOW, kvw), lambda b, j: (b, 0, 0))],
        out_shape=[jax.ShapeDtypeStruct((batch * seq, qw), F32),
                   jax.ShapeDtypeStruct((batch, WINDOW, kvw), F32),
                   jax.ShapeDtypeStruct((batch, WINDOW, kvw), F32)],
        scratch_shapes=[pltpu.VMEM((WINDOW, kvw), F32), pltpu.VMEM((WINDOW, kvw), F32)],
        compiler_params=_params("parallel", "arbitrary"),
        name="attn_prompt",
    )(sinks, proj, proj, c_tab, s_tab, g_qn.reshape(1, HEAD_C), g_kn.reshape(1, HEAD_C))


def _attn_sample_kernel(sinks_ref, q_ref, kv_ref, ck_ref, cv_ref, c_ref, s_ref, gq_ref, gk_ref,
                        o_ref, newk_ref, newv_ref, *, kv_heads, group, t_new):
    kvw = kv_heads * HEAD_C
    c_tab = c_ref[...]
    s_tab = s_ref[...]
    rows = group * t_new
    row = lax.broadcasted_iota(jnp.int32, (rows, 2 * WINDOW), 0)
    cidx = lax.broadcasted_iota(jnp.int32, (rows, 2 * WINDOW), 1)
    r = lax.rem(row, t_new)
    valid = ((cidx < WINDOW) & (cidx > r)) | ((cidx >= WINDOW) & (cidx - WINDOW <= r))
    head_in_group = lax.broadcasted_iota(jnp.int32, (rows, 1), 0) // t_new
    filler = jnp.zeros((WINDOW - t_new, HEAD_C), F32)

    for g in range(kv_heads):
        k_new = _norm_rope(kv_ref[:, g * HEAD_C:(g + 1) * HEAD_C], gk_ref[...], c_tab, s_tab)
        v_new = kv_ref[:, kvw + g * HEAD_C:kvw + (g + 1) * HEAD_C]
        k_old = ck_ref[:, g * HEAD_C:(g + 1) * HEAD_C]
        v_old = cv_ref[:, g * HEAD_C:(g + 1) * HEAD_C]
        keys = jnp.concatenate([k_old, k_new, filler], axis=0)
        vals = jnp.concatenate([v_old, v_new, filler], axis=0)
        heads = [g * group + i for i in range(group)]
        q_g = jnp.concatenate([_norm_rope(q_ref[:, h * HEAD_C:(h + 1) * HEAD_C], gq_ref[...], c_tab, s_tab)
                               for h in heads], axis=0)
        sink = jnp.zeros((rows, 1), F32)
        for i, h in enumerate(heads):
            sink = jnp.where(head_in_group == i, sinks_ref[h], sink)
        o_g = _pv(_softmax_sink(_qk_logits(q_g, keys), valid, sink), vals)
        for i, h in enumerate(heads):
            o_ref[:, h * HEAD_C:(h + 1) * HEAD_C] = o_g[i * t_new:(i + 1) * t_new, :]
        newk_ref[0:WINDOW - t_new, g * HEAD_C:(g + 1) * HEAD_C] = k_old[t_new:, :]
        newk_ref[WINDOW - t_new:, g * HEAD_C:(g + 1) * HEAD_C] = k_new
        newv_ref[0:WINDOW - t_new, g * HEAD_C:(g + 1) * HEAD_C] = v_old[t_new:, :]
        newv_ref[WINDOW - t_new:, g * HEAD_C:(g + 1) * HEAD_C] = v_new


def attn_sample(proj, cache_k, cache_v, sinks, g_qn, g_kn, *, batch, seq, start, row0, q_col, k_col,
                q_heads, kv_heads):
    qw = q_heads * HEAD_C
    kvw = kv_heads * HEAD_C
    c_tab, s_tab = _rope_tables(start + jnp.arange(seq))
    kern = functools.partial(_attn_sample_kernel, kv_heads=kv_heads, group=q_heads // kv_heads, t_new=seq)
    blk0 = row0 // seq
    return pl.pallas_call(
        kern,
        grid=(batch,),
        in_specs=[pl.BlockSpec(memory_space=pltpu.SMEM),
                  pl.BlockSpec((seq, qw), lambda b: (blk0 + b, q_col // qw)),
                  pl.BlockSpec((seq, 2 * kvw), lambda b: (blk0 + b, k_col // (2 * kvw))),
                  pl.BlockSpec((None, WINDOW, kvw), lambda b: (b, 0, 0)),
                  pl.BlockSpec((None, WINDOW, kvw), lambda b: (b, 0, 0)),
                  pl.BlockSpec((seq, HEAD_C), lambda b: (0, 0)),
                  pl.BlockSpec((seq, HEAD_C), lambda b: (0, 0)),
                  pl.BlockSpec((1, HEAD_C), lambda b: (0, 0)),
                  pl.BlockSpec((1, HEAD_C), lambda b: (0, 0))],
        out_specs=[pl.BlockSpec((seq, qw), lambda b: (b, 0)),
                   pl.BlockSpec((None, WINDOW, kvw), lambda b: (b, 0, 0)),
                   pl.BlockSpec((None, WINDOW, kvw), lambda b: (b, 0, 0))],
        out_shape=[jax.ShapeDtypeStruct((batch * seq, qw), F32),
                   jax.ShapeDtypeStruct((batch, WINDOW, kvw), F32),
                   jax.ShapeDtypeStruct((batch, WINDOW, kvw), F32)],
        compiler_params=_params("parallel"),
        name="attn_sample",
    )(sinks, proj, proj, cache_k, cache_v, c_tab, s_tab, g_qn.reshape(1, HEAD_C), g_kn.reshape(1, HEAD_C))


def _ffn_up_kernel(te_ref, tv_ref, x_ref, w1_ref, w3_ref, o_ref, xb_ref):
    i = pl.program_id(0)

    @pl.when(pl.program_id(1) == 0)
    def _():
        xb_ref[...] = x_ref[...].astype(BF16)

    @pl.when(tv_ref[i] > 0)
    def _():
        x = xb_ref[...]
        a = jnp.dot(x, w1_ref[...].astype(BF16), preferred_element_type=F32)
        b = jnp.dot(x, w3_ref[...].astype(BF16), preferred_element_type=F32)
        o_ref[...] = (_silu(a) * b).astype(o_ref.dtype)

    @pl.when(tv_ref[i] == 0)
    def _():
        o_ref[...] = jnp.zeros_like(o_ref)


def ffn_up(x, w1, w3, tile_expert, tile_valid, tm):
    m, k = x.shape
    f = w1.shape[-1]
    tf = _pick(f, (256, 128))
    nf = f // tf

    def w_map(i, j, te, tv):
        return (te[i], 0, jnp.where(tv[i] > 0, j, nf - 1))

    return pl.pallas_call(
        _ffn_up_kernel,
        grid_spec=pltpu.PrefetchScalarGridSpec(
            num_scalar_prefetch=2,
            grid=(m // tm, nf),
            in_specs=[pl.BlockSpec((tm, k), lambda i, j, te, tv: (i, 0), pipeline_mode=pl.Buffered(1)),
                      pl.BlockSpec((None, k, tf), w_map),
                      pl.BlockSpec((None, k, tf), w_map)],
            out_specs=pl.BlockSpec((tm, tf), lambda i, j, te, tv: (i, j)),
            scratch_shapes=[pltpu.VMEM((tm, k), BF16)]),
        out_shape=jax.ShapeDtypeStruct((m, f), BF16),
        compiler_params=_params("parallel", "arbitrary"),
        name="ffn_up",
    )(tile_expert, tile_valid, x, w1, w3)


def _ffn_down_kernel(te_ref, tv_ref, g_ref, w2_ref, *rest, n_k, with_residual):
    if with_residual:
        r_ref, o_ref, acc_ref = rest
    else:
        o_ref, acc_ref = rest
    i = pl.program_id(0)
    kk = pl.program_id(2)

    @pl.when(kk == 0)
    def _():
        acc_ref[...] = jnp.zeros_like(acc_ref)

    @pl.when(tv_ref[i] > 0)
    def _():
        acc_ref[...] += jnp.dot(g_ref[...], w2_ref[...].astype(BF16), preferred_element_type=F32)

    @pl.when(kk == n_k - 1)
    def _():
        if with_residual:
            o_ref[...] = r_ref[...] + acc_ref[...]
        else:
            o_ref[...] = acc_ref[...]


def ffn_down(g, w2, tile_expert, tile_valid, tm, residual=None):
    m, f = g.shape
    n = w2.shape[-1]
    tn = _pick(n, (1024, 512, 256, 128))
    tk = _pick(f, (1024, 896, 512, 256, 128))
    n_k = f // tk

    def w_map(i, j, kk, te, tv):
        return (te[i], jnp.where(tv[i] > 0, kk, n_k - 1), j)

    in_specs = [pl.BlockSpec((tm, tk), lambda i, j, kk, te, tv: (i, jnp.where(tv[i] > 0, kk, n_k - 1))),
                pl.BlockSpec((None, tk, tn), w_map)]
    args = [tile_expert, tile_valid, g, w2]
    if residual is not None:
        in_specs.append(pl.BlockSpec((tm, tn), lambda i, j, kk, te, tv: (i, j)))
        args.append(residual)
    kern = functools.partial(_ffn_down_kernel, n_k=n_k, with_residual=residual is not None)
    return pl.pallas_call(
        kern,
        grid_spec=pltpu.PrefetchScalarGridSpec(
            num_scalar_prefetch=2,
            grid=(m // tm, n // tn, n_k),
            in_specs=in_specs,
            out_specs=pl.BlockSpec((tm, tn), lambda i, j, kk, te, tv: (i, j)),
            scratch_shapes=[pltpu.VMEM((tm, tn), F32)]),
        out_shape=jax.ShapeDtypeStruct((m, n), F32),
        compiler_params=_params("parallel", "arbitrary", "arbitrary"),
        name="ffn_down",
    )(*args)


def _router_kernel(x_ref, g_ref, w_ref, h_ref, idx_ref, gate_ref, *, n_experts):
    x = x_ref[...]
    h = x * lax.rsqrt(jnp.mean(x * x, axis=-1, keepdims=True) + RMS_EPS) * g_ref[...]
    h_ref[...] = h
    logits = jnp.dot(h, w_ref[...], precision=HIGHEST, preferred_element_type=F32)
    lane = lax.broadcasted_iota(jnp.int32, logits.shape, 1)
    logits = jnp.where(lane < n_experts, logits, -jnp.inf)
    v1 = jnp.max(logits, axis=-1, keepdims=True)
    i1 = jnp.min(jnp.where(logits == v1, lane, LANES), axis=-1, keepdims=True)
    rest = jnp.where(lane == i1, -jnp.inf, logits)
    v2 = jnp.max(rest, axis=-1, keepdims=True)
    i2 = jnp.min(jnp.where(rest == v2, lane, LANES), axis=-1, keepdims=True)
    e2 = jnp.exp(v2 - v1)
    g1 = 1.0 / (1.0 + e2)
    g2 = e2 / (1.0 + e2)
    idx_ref[...] = jnp.where(lane == 0, i1, jnp.where(lane == 1, i2, 0))
    gate_ref[...] = jnp.where(lane == 0, g1, jnp.where(lane == 1, g2, 0.0))


def router(x, g_ffn, w_router):
    m, d = x.shape
    n_experts = w_router.shape[-1]
    tm = _pick(m, (256, 128, 64, 32, 16))
    w_pad = jnp.zeros((d, LANES), F32).at[:, :n_experts].set(w_router)
    kern = functools.partial(_router_kernel, n_experts=n_experts)
    h, idx, gate = pl.pallas_call(
        kern,
        grid=(m // tm,),
        in_specs=[pl.BlockSpec((tm, d), lambda i: (i, 0)),
                  pl.BlockSpec((1, d), lambda i: (0, 0)),
                  pl.BlockSpec((d, LANES), lambda i: (0, 0))],
        out_specs=[pl.BlockSpec((tm, d), lambda i: (i, 0)),
                   pl.BlockSpec((tm, LANES), lambda i: (i, 0)),
                   pl.BlockSpec((tm, LANES), lambda i: (i, 0))],
        out_shape=[jax.ShapeDtypeStruct((m, d), F32),
                   jax.ShapeDtypeStruct((m, LANES), jnp.int32),
                   jax.ShapeDtypeStruct((m, LANES), F32)],
        compiler_params=_params("parallel"),
        name="router",
    )(x, g_ffn.reshape(1, d), w_pad)
    return h, idx[:, :TOP_K], gate[:, :TOP_K]


def _gather_rows_kernel(idx_ref, src_ref, o_ref, sem, *, rows):
    base = pl.program_id(0) * rows

    def row_copy(r):
        return pltpu.make_async_copy(src_ref.at[pl.ds(idx_ref[base + r], 1)], o_ref.at[pl.ds(r, 1)], sem)

    def start(r, carry):
        row_copy(r).start()
        return carry

    def wait(r, carry):
        row_copy(r).wait()
        return carry

    lax.fori_loop(0, rows, start, 0)
    lax.fori_loop(0, rows, wait, 0)


def gather_rows(src, idx):
    n = idx.shape[0]
    d = src.shape[1]
    rows = _pick(n, (256, 128, 64, 32, 16, 8))
    return pl.pallas_call(
        functools.partial(_gather_rows_kernel, rows=rows),
        grid_spec=pltpu.PrefetchScalarGridSpec(
            num_scalar_prefetch=1,
            grid=(n // rows,),
            in_specs=[pl.BlockSpec(memory_space=pl.ANY)],
            out_specs=pl.BlockSpec((rows, d), lambda i, idx: (i, 0)),
            scratch_shapes=[pltpu.SemaphoreType.DMA(())]),
        out_shape=jax.ShapeDtypeStruct((n, d), src.dtype),
        compiler_params=_params("arbitrary"),
        name="gather_rows",
    )(idx, src)


def _combine_kernel(x_ref, ya_ref, yb_ref, gate_ref, o_ref):
    gate = gate_ref[...]
    ffn = gate[:, 0:1] * ya_ref[...] + gate[:, 1:2] * yb_ref[...]
    o_ref[...] = x_ref[...] + ffn


def moe_combine(x, y_first, y_second, gate):
    m, d = x.shape
    tm = _pick(m, (256, 128, 64, 32, 16))
    gate_pad = jnp.zeros((m, LANES), F32).at[:, :TOP_K].set(gate)
    row = pl.BlockSpec((tm, d), lambda i: (i, 0))
    return pl.pallas_call(
        _combine_kernel,
        grid=(m // tm,),
        in_specs=[row, row, row, pl.BlockSpec((tm, LANES), lambda i: (i, 0))],
        out_specs=row,
        out_shape=jax.ShapeDtypeStruct((m, d), F32),
        compiler_params=_params("parallel"),
        name="moe_combine",
    )(x, y_first, y_second, gate_pad)


def moe_ffn(x, g_ffn, w_router, w1, w3, w2, which):
    m, d = x.shape
    n_experts = w_router.shape[-1]
    w1, w3, w2 = (w.reshape((-1,) + w.shape[2:]) for w in (w1, w3, w2))
    h, top_i, top_w = router(x, g_ffn, w_router)

    tm = _pick(m, (1056, 1024, 512, 256, 128, 64, 32, 16))
    n_assign = m * TOP_K
    n_tiles = (n_assign + n_experts * (tm - 1)) // tm
    flat_e = top_i.reshape(-1)
    counts = jnp.sum(jax.nn.one_hot(flat_e, n_experts, dtype=jnp.int32), axis=0)
    tiles_per = (counts + tm - 1) // tm
    tile_end = jnp.cumsum(tiles_per)
    group_start = (tile_end - tiles_per) * tm
    order = jnp.argsort(flat_e, stable=True)
    sorted_e = flat_e[order]
    rank = jnp.arange(n_assign, dtype=jnp.int32) - (jnp.cumsum(counts) - counts)[sorted_e]
    slot_sorted = group_start[sorted_e] + rank
    slot = jnp.zeros((n_assign,), jnp.int32).at[order].set(slot_sorted)
    row_token = jnp.zeros((n_tiles * tm,), jnp.int32).at[slot_sorted].set((order // TOP_K).astype(jnp.int32))
    tile_ids = jnp.arange(n_tiles, dtype=jnp.int32)
    tile_valid = (tile_ids < tile_end[-1]).astype(jnp.int32)
    tile_expert = jnp.minimum(jnp.searchsorted(tile_end, tile_ids, side="right"), n_experts - 1).astype(jnp.int32)
    tile_expert = jnp.where(tile_valid > 0, tile_expert, tile_expert[jnp.maximum(tile_end[-1] - 1, 0)])
    tile_expert = tile_expert + which * n_experts

    xs = gather_rows(h, row_token)
    g = ffn_up(xs, w1, w3, tile_expert, tile_valid, tm)
    ys = ffn_down(g, w2, tile_expert, tile_valid, tm)
    slot = slot.reshape(m, TOP_K)
    y_first = gather_rows(ys, slot[:, 0])
    y_second = gather_rows(ys, slot[:, 1])
    return moe_combine(x, y_first, y_second, top_w)


def dense_ffn(x, g_ffn, w1, w3, w2, which):
    m = x.shape[0]
    h = rmsnorm_rows(x, g_ffn, BF16)
    tm = _pick(m, (1056, 1024, 512, 256, 128, 64, 32, 16))
    tile_expert = jnp.full((m // tm,), which, jnp.int32)
    tile_valid = jnp.ones((m // tm,), jnp.int32)
    g = ffn_up(h, w1, w3, tile_expert, tile_valid, tm)
    return ffn_down(g, w2, tile_expert, tile_valid, tm, residual=x)


def kernel(x_prompt, x_sample, state_hgrn, state_conv, cache_k, cache_v, lb_param, g_mix, w_in, g_hgrn, w_out_a, w_dw, b_dw, ln_g, ln_b, w_out_b, g_qn, g_kn, sinks, w_out_c, w_o, g_ffn, w1_dense, w3_dense, w2_dense, w_router, w1_moe, w3_moe, w2_moe):
    bp, tp, d = x_prompt.shape
    bs, ts, _ = x_sample.shape
    depth = w_in.shape[0]
    heads_a = state_hgrn.shape[2]
    channels_b = state_conv.shape[-1]
    conv_w = w_dw.shape[1]
    kv_heads = cache_k.shape[3]
    q_heads = sinks.shape[1]
    n_prompt = bp * tp
    past_len = PAST_LEN

    a_cols = heads_a * HEAD_A
    lin_col = 4 * a_cols
    gate_b_col = lin_col + channels_b
    q_col = gate_b_col + channels_b
    k_col = q_col + q_heads * HEAD_C
    merge_gate_col = k_col + 2 * kv_heads * HEAD_C

    p_soft = jax.nn.softmax(lb_param.astype(F32), axis=0)
    lower = jnp.cumsum(p_soft, axis=0) - p_soft[0:1]

    x = jnp.concatenate([x_prompt.reshape(n_prompt, d), x_sample.reshape(bs * ts, d)], axis=0)
    zero_state = jnp.zeros((bp, heads_a, HEAD_A, HEAD_A), F32)
    zero_hist = jnp.zeros((bp, HIST_ROWS, channels_b), F32)
    outs = {name: [] for name in ("hp", "cp", "kp", "vp", "hs", "cs", "ks", "vs")}

    for l in range(depth):
        h = rmsnorm_rows(x, g_mix[l], BF16)
        proj = gemm(h, w_in, l)

        oa_p, s_p = hgrn_mixer(proj, lower[l], g_hgrn[l], zero_state, batch=bp, seq=tp,
                               chunk=64 if tp % 64 == 0 else tp, row0=0, heads=heads_a)
        oa_s, s_s = hgrn_mixer(proj, lower[l], g_hgrn[l], state_hgrn[l], batch=bs, seq=ts,
                               chunk=64 if ts % 64 == 0 else ts, row0=n_prompt, heads=heads_a)

        hist_s = jnp.pad(state_conv[l], ((0, 0), (HIST_ROWS - (conv_w - 1), 0), (0, 0)))
        cb_p, c_p = conv_mixer(proj, zero_hist, w_dw[l], b_dw[l], batch=bp, seq=tp,
                               tb=_pick(tp, (128, 64, 32, 16, 8)), row0=0, lin_col=lin_col, gate_col=gate_b_col)
        cb_s, c_s = conv_mixer(proj, hist_s, w_dw[l], b_dw[l], batch=bs, seq=ts,
                               tb=_pick(ts, (128, 64, 32, 16, 8)), row0=n_prompt, lin_col=lin_col,
                               gate_col=gate_b_col)
        o_b = layernorm_silu_rows(jnp.concatenate([cb_p, cb_s], axis=0), ln_g[l], ln_b[l], BF16)

        oc_p, k_p, v_p = attn_prompt(proj, sinks[l], g_qn[l], g_kn[l], batch=bp, seq=tp, q_col=q_col, k_col=k_col,
                                     q_heads=q_heads, kv_heads=kv_heads)
        oc_s, k_s, v_s = attn_sample(proj, cache_k[l].reshape(bs, WINDOW, kv_heads * HEAD_C),
                                     cache_v[l].reshape(bs, WINDOW, kv_heads * HEAD_C), sinks[l], g_qn[l], g_kn[l],
                                     batch=bs, seq=ts, start=past_len, row0=n_prompt, q_col=q_col, k_col=k_col,
                                     q_heads=q_heads, kv_heads=kv_heads)

        o_a = jnp.concatenate([oa_p, oa_s], axis=0)
        o_c = jnp.concatenate([oc_p, oc_s], axis=0)
        mixed = gated_merge(o_a, o_b, o_c, w_out_a, w_out_b, w_out_c, proj, l, merge_gate_col, d)
        x = gemm(mixed, w_o, l, residual=x)

        j = l // 2
        if l % 2 == 0:
            x = dense_ffn(x, g_ffn[l], w1_dense, w3_dense, w2_dense, j)
        else:
            x = moe_ffn(x, g_ffn[l], w_router[j], w1_moe, w3_moe, w2_moe, j)

        outs["hp"].append(s_p)
        outs["cp"].append(c_p)
        outs["kp"].append(k_p.reshape(bp, WINDOW, kv_heads, HEAD_C))
        outs["vp"].append(v_p.reshape(bp, WINDOW, kv_heads, HEAD_C))
        outs["hs"].append(s_s)
        outs["cs"].append(c_s)
        outs["ks"].append(k_s.reshape(bs, WINDOW, kv_heads, HEAD_C))
        outs["vs"].append(v_s.reshape(bs, WINDOW, kv_heads, HEAD_C))

    y_prompt = x[:n_prompt].reshape(bp, tp, d)
    y_sample = x[n_prompt:].reshape(bs, ts, d)
    return (y_prompt, y_sample,
            jnp.stack(outs["hp"]), jnp.stack(outs["cp"]), jnp.stack(outs["kp"]), jnp.stack(outs["vp"]),
            jnp.stack(outs["hs"]), jnp.stack(outs["cs"]), jnp.stack(outs["ks"]), jnp.stack(outs["vs"]))
```

```python
import functools

import jax
import jax.numpy as jnp
from jax import lax
from jax.experimental import pallas as pl
from jax.experimental.pallas import tpu as pltpu

F32 = jnp.float32
BF16 = jnp.bfloat16

RMS_EPS = 1e-6
LN_EPS = 1e-5
F_MIN = 1e-30
NEG_BIG = -1e30
ROPE_THETA = 500000.0
HEAD_A = 128
HEAD_C = 64
WINDOW = 128
PAST_LEN = 16384
TOP_K = 2
HIGHEST = lax.Precision.HIGHEST

V7X_VMEM_LIMIT_BYTES = 56 * 1024 * 1024
LANES = 128
SUBLANES = 8


def _pick(n, candidates):
    for c in candidates:
        if n % c == 0:
            return c
    raise ValueError(f"no tile in {candidates} divides {n}")


def _params(*semantics):
    return pltpu.CompilerParams(dimension_semantics=semantics, vmem_limit_bytes=V7X_VMEM_LIMIT_BYTES)


def _silu(x):
    return x * jax.nn.sigmoid(x)


def _rmsnorm_kernel(x_ref, g_ref, o_ref):
    x = x_ref[...]
    ms = jnp.mean(x * x, axis=-1, keepdims=True)
    o_ref[...] = (x * lax.rsqrt(ms + RMS_EPS) * g_ref[...]).astype(o_ref.dtype)


def rmsnorm_rows(x, g, out_dtype):
    m, d = x.shape
    tm = _pick(m, (256, 128, 64, 32, 16))
    return pl.pallas_call(
        _rmsnorm_kernel,
        grid=(m // tm,),
        in_specs=[pl.BlockSpec((tm, d), lambda i: (i, 0)), pl.BlockSpec((1, d), lambda i: (0, 0))],
        out_specs=pl.BlockSpec((tm, d), lambda i: (i, 0)),
        out_shape=jax.ShapeDtypeStruct((m, d), out_dtype),
        compiler_params=_params("parallel"),
        name="rmsnorm_rows",
    )(x, g.reshape(1, d))


def _ln_silu_kernel(x_ref, g_ref, b_ref, o_ref):
    x = x_ref[...]
    mu = jnp.mean(x, axis=-1, keepdims=True)
    xc = x - mu
    var = jnp.mean(xc * xc, axis=-1, keepdims=True)
    y = xc * lax.rsqrt(var + LN_EPS) * g_ref[...] + b_ref[...]
    o_ref[...] = _silu(y).astype(o_ref.dtype)


def layernorm_silu_rows(x, g, b, out_dtype):
    m, d = x.shape
    tm = _pick(m, (256, 128, 64, 32, 16))
    return pl.pallas_call(
        _ln_silu_kernel,
        grid=(m // tm,),
        in_specs=[pl.BlockSpec((tm, d), lambda i: (i, 0)),
                  pl.BlockSpec((1, d), lambda i: (0, 0)),
                  pl.BlockSpec((1, d), lambda i: (0, 0))],
        out_specs=pl.BlockSpec((tm, d), lambda i: (i, 0)),
        out_shape=jax.ShapeDtypeStruct((m, d), out_dtype),
        compiler_params=_params("parallel"),
        name="layernorm_silu_rows",
    )(x, g.reshape(1, d), b.reshape(1, d))


def _gemm_kernel(x_ref, w_ref, o_ref):
    o_ref[...] = jnp.dot(x_ref[...].astype(BF16), w_ref[...].astype(BF16), preferred_element_type=F32)


def _gemm_residual_kernel(x_ref, w_ref, r_ref, o_ref):
    o_ref[...] = r_ref[...] + jnp.dot(x_ref[...].astype(BF16), w_ref[...].astype(BF16),
                                      preferred_element_type=F32)


def gemm(x, w, layer, residual=None):
    m, k = x.shape
    n = w.shape[-1]
    tm = _pick(m, (1056, 1024, 512, 256, 128, 64, 32, 16))
    tn = _pick(n, (512, 256, 128))
    in_specs = [pl.BlockSpec((tm, k), lambda i, j: (i, 0)),
                pl.BlockSpec((None, k, tn), lambda i, j: (layer, 0, j))]
    args = [x, w]
    body = _gemm_kernel
    if residual is not None:
        in_specs.append(pl.BlockSpec((tm, tn), lambda i, j: (i, j)))
        args.append(residual)
        body = _gemm_residual_kernel
    return pl.pallas_call(
        body,
        grid=(m // tm, n // tn),
        in_specs=in_specs,
        out_specs=pl.BlockSpec((tm, tn), lambda i, j: (i, j)),
        out_shape=jax.ShapeDtypeStruct((m, n), F32),
        compiler_params=_params("parallel", "arbitrary"),
        name="gemm_residual" if residual is not None else "gemm",
    )(*args)


def _merge_kernel(oa_ref, ob_ref, oc_ref, wa_ref, wb_ref, wc_ref, ga_ref, gb_ref, gc_ref, o_ref):
    def branch(o_ref_, w_ref_, g_ref_):
        y = jnp.dot(o_ref_[...].astype(BF16), w_ref_[...].astype(BF16), preferred_element_type=F32)
        return jax.nn.sigmoid(g_ref_[...]) * y

    mixed = branch(oa_ref, wa_ref, ga_ref) + branch(ob_ref, wb_ref, gb_ref) + branch(oc_ref, wc_ref, gc_ref)
    o_ref[...] = mixed.astype(o_ref.dtype)


def gated_merge(o_a, o_b, o_c, w_out_a, w_out_b, w_out_c, proj, layer, gate_col, d_model):
    m = o_a.shape[0]
    tm = _pick(m, (1056, 1024, 512, 256, 128, 64, 32, 16))
    tn = next(c for c in (256, 128) if d_model % c == 0 and gate_col % c == 0)
    gate_blk = gate_col // tn
    per_gate = d_model // tn

    def o_spec(o):
        return pl.BlockSpec((tm, o.shape[1]), lambda i, j: (i, 0), pipeline_mode=pl.Buffered(1))

    def w_spec(w):
        return pl.BlockSpec((None, w.shape[1], tn), lambda i, j: (layer, 0, j))

    def g_spec(which):
        return pl.BlockSpec((tm, tn), lambda i, j: (i, gate_blk + which * per_gate + j))

    return pl.pallas_call(
        _merge_kernel,
        grid=(m // tm, d_model // tn),
        in_specs=[o_spec(o_a), o_spec(o_b), o_spec(o_c), w_spec(w_out_a), w_spec(w_out_b), w_spec(w_out_c),
                  g_spec(0), g_spec(1), g_spec(2)],
        out_specs=pl.BlockSpec((tm, tn), lambda i, j: (i, j)),
        out_shape=jax.ShapeDtypeStruct((m, d_model), BF16),
        compiler_params=_params("parallel", "arbitrary"),
        name="gated_merge",
    )(o_a, o_b, o_c, w_out_a, w_out_b, w_out_c, proj, proj, proj)


def _cumsum_rows(x):
    n = x.shape[0]
    row = lax.broadcasted_iota(jnp.int32, x.shape, 0)
    shift = 1
    while shift < n:
        x = x + jnp.where(row >= shift, pltpu.roll(x, shift, axis=0), 0.0)
        shift *= 2
    return x


def _bf16_dot_t(a, b):
    return lax.dot_general(a.astype(BF16), b.astype(BF16), (((1,), (1,)), ((), ())), preferred_element_type=F32)


def _hgrn_head(q_in, f_in, v, gate_in, lb, gain, st, *, chunk):
    sig = jax.nn.sigmoid(f_in)
    f_gate = lb + (1.0 - lb) * sig
    log_f = jnp.log(jnp.maximum(f_gate, F_MIN))
    k = (1.0 - lb) * (1.0 - sig)
    q = _silu(q_in)
    b = _cumsum_rows(log_f)

    o = _bf16_dot_t(q * jnp.exp(b), st)

    n_sub = chunk // SUBLANES
    q3, k3, b3, v3 = (a.reshape(n_sub, SUBLANES, HEAD_A) for a in (q, k, b, v))
    r8 = lax.broadcasted_iota(jnp.int32, (n_sub, SUBLANES, 1), 1)
    acc = jnp.zeros((n_sub, SUBLANES, HEAD_A), F32)
    for s8 in range(SUBLANES):
        causal = r8 >= s8
        decay = jnp.exp(jnp.where(causal, b3 - b3[:, s8:s8 + 1, :], 0.0))
        score = jnp.sum(jnp.where(causal, q3 * k3[:, s8:s8 + 1, :] * decay, 0.0), axis=-1, keepdims=True)
        acc = acc + score * v3[:, s8:s8 + 1, :]
    o = o + acc.reshape(chunk, HEAD_A)

    row = lax.broadcasted_iota(jnp.int32, (chunk, 1), 0)
    scores = None
    m = SUBLANES
    while 2 * m <= chunk:
        n_blk = chunk // (2 * m)
        ref_row = b.reshape(n_blk, 2 * m, HEAD_A)[:, m - 1:m, :]
        beta = jnp.broadcast_to(ref_row, (n_blk, 2 * m, HEAD_A)).reshape(chunk, HEAD_A)
        second = lax.rem(row, 2 * m) >= m
        q_m = jnp.where(second, q * jnp.exp(jnp.where(second, b - beta, 0.0)), 0.0)
        k_m = jnp.where(second, 0.0, k * jnp.exp(jnp.where(second, 0.0, beta - b)))
        a = _bf16_dot_t(q_m, k_m)
        if n_blk > 1:
            blk_r = lax.broadcasted_iota(jnp.int32, (chunk, chunk), 0) // (2 * m)
            blk_c = lax.broadcasted_iota(jnp.int32, (chunk, chunk), 1) // (2 * m)
            a = jnp.where(blk_r == blk_c, a, 0.0)
        scores = a if scores is None else scores + a
        m *= 2
    if scores is not None:
        o = o + jnp.dot(scores.astype(BF16), v.astype(BF16), preferred_element_type=F32)

    o_n = o * lax.rsqrt(jnp.mean(o * o, axis=-1, keepdims=True) + RMS_EPS) * gain
    out = o_n * _silu(gate_in)

    b_last = b[chunk - 1:chunk, :]
    k_dec = k * jnp.exp(b_last - b)
    st_new = jnp.exp(b_last) * st + lax.dot_general(v.astype(BF16), k_dec.astype(BF16), (((0,), (0,)), ((), ())),
                                                    preferred_element_type=F32)
    return out, st_new


def _hgrn_kernel(q_ref, f_ref, i_ref, g_ref, lb_ref, gh_ref, s0_ref, *rest, chunk, n_chunks, heads_per_step,
                 aliased):
    o_ref, sfin_ref, st_ref = rest[1:] if aliased else rest
    c = pl.program_id(2)

    @pl.when(c == 0)
    def _():
        for hh in range(heads_per_step):
            st_ref[hh] = s0_ref[hh].T

    for hh in range(heads_per_step):
        cols = slice(hh * HEAD_A, (hh + 1) * HEAD_A)
        out, st_new = _hgrn_head(q_ref[:, cols], f_ref[:, cols], i_ref[:, cols], g_ref[:, cols], lb_ref[:, cols],
                                 gh_ref[...], st_ref[hh], chunk=chunk)
        o_ref[:, cols] = out
        st_ref[hh] = st_new

        @pl.when(c == n_chunks - 1)
        def _(hh=hh, st_new=st_new):
            sfin_ref[hh] = st_new.T


def hgrn_mixer(proj, lb, g_hgrn, s0, *, batch, seq, chunk, row0, heads, total_rows, prev_out=None):
    n_chunks = seq // chunk
    blk0 = row0 // chunk
    hps = _pick(heads, (4, 2, 1))
    width = hps * HEAD_A

    def p_spec(slab):
        return pl.BlockSpec((chunk, width), lambda b, h, c: (blk0 + b * n_chunks + c, slab * (heads // hps) + h))

    in_specs = [p_spec(0), p_spec(1), p_spec(2), p_spec(3),
                pl.BlockSpec((1, width), lambda b, h, c: (0, h)),
                pl.BlockSpec((1, HEAD_A), lambda b, h, c: (0, 0)),
                pl.BlockSpec((None, hps, HEAD_A, HEAD_A), lambda b, h, c: (b, h, 0, 0))]
    args = [proj, proj, proj, proj, lb.reshape(1, heads * HEAD_A), g_hgrn.reshape(1, HEAD_A), s0]
    aliases = {}
    if prev_out is not None:
        in_specs.append(pl.BlockSpec(memory_space=pl.ANY))
        args.append(prev_out)
        aliases = {len(args) - 1: 0}
    kern = functools.partial(_hgrn_kernel, chunk=chunk, n_chunks=n_chunks, heads_per_step=hps,
                             aliased=prev_out is not None)
    return pl.pallas_call(
        kern,
        grid=(batch, heads // hps, n_chunks),
        in_specs=in_specs,
        out_specs=[pl.BlockSpec((chunk, width), lambda b, h, c: (blk0 + b * n_chunks + c, h)),
                   pl.BlockSpec((None, hps, HEAD_A, HEAD_A), lambda b, h, c: (b, h, 0, 0))],
        out_shape=[jax.ShapeDtypeStruct((total_rows, heads * HEAD_A), F32),
                   jax.ShapeDtypeStruct((batch, heads, HEAD_A, HEAD_A), F32)],
        scratch_shapes=[pltpu.VMEM((hps, HEAD_A, HEAD_A), F32)],
        input_output_aliases=aliases,
        compiler_params=_params("parallel", "parallel", "arbitrary"),
        name="hgrn_mixer",
    )(*args)


HIST_ROWS = 32


def _conv_kernel(ul_ref, ug_ref, hist_ref, w_ref, bias_ref, *rest, tb, conv_w, aliased):
    o_ref, state_ref, buf_ref = rest[1:] if aliased else rest
    t = pl.program_id(2)

    @pl.when(t == 0)
    def _():
        buf_ref[0:HIST_ROWS, :] = hist_ref[...]

    u = ul_ref[...] * jax.nn.sigmoid(ug_ref[...])
    buf_ref[HIST_ROWS:HIST_ROWS + tb, :] = u
    first = HIST_ROWS - (conv_w - 1)
    acc = jnp.zeros(u.shape, F32)
    for j in range(conv_w):
        acc = acc + w_ref[j:j + 1, :] * buf_ref[first + j:first + j + tb, :]
    o_ref[...] = acc + bias_ref[...]
    state_ref[...] = buf_ref[tb + first:tb + HIST_ROWS, :]
    tail = buf_ref[tb:tb + HIST_ROWS, :]
    buf_ref[0:HIST_ROWS, :] = tail


def conv_mixer(proj, hist, w_dw, b_dw, *, batch, seq, tb, row0, lin_col, gate_col, total_rows, prev_out=None):
    conv_w, channels = w_dw.shape
    cw = _pick(channels, (256, 128))
    n_t = seq // tb
    blk0 = row0 // tb
    in_specs = [pl.BlockSpec((tb, cw), lambda b, c, t: (blk0 + b * n_t + t, lin_col // cw + c)),
                pl.BlockSpec((tb, cw), lambda b, c, t: (blk0 + b * n_t + t, gate_col // cw + c)),
                pl.BlockSpec((None, HIST_ROWS, cw), lambda b, c, t: (b, 0, c)),
                pl.BlockSpec((conv_w, cw), lambda b, c, t: (0, c)),
                pl.BlockSpec((1, cw), lambda b, c, t: (0, c))]
    args = [proj, proj, hist, w_dw, b_dw.reshape(1, channels)]
    aliases = {}
    if prev_out is not None:
        in_specs.append(pl.BlockSpec(memory_space=pl.ANY))
        args.append(prev_out)
        aliases = {len(args) - 1: 0}
    kern = functools.partial(_conv_kernel, tb=tb, conv_w=conv_w, aliased=prev_out is not None)
    return pl.pallas_call(
        kern,
        grid=(batch, channels // cw, n_t),
        in_specs=in_specs,
        out_specs=[pl.BlockSpec((tb, cw), lambda b, c, t: (blk0 + b * n_t + t, c)),
                   pl.BlockSpec((None, conv_w - 1, cw), lambda b, c, t: (b, 0, c))],
        out_shape=[jax.ShapeDtypeStruct((total_rows, channels), F32),
                   jax.ShapeDtypeStruct((batch, conv_w - 1, channels), F32)],
        scratch_shapes=[pltpu.VMEM((HIST_ROWS + tb, cw), F32)],
        input_output_aliases=aliases,
        compiler_params=_params("parallel", "parallel", "arbitrary"),
        name="conv_mixer",
    )(*args)


def _rope_tables(pos):
    rot = HEAD_C // 4
    inv_freq = jnp.float32(ROPE_THETA) ** (-jnp.arange(0, rot, 2, dtype=F32) / rot)
    ang = pos.astype(F32)[:, None] * inv_freq[None, :]
    cos, sin = jnp.cos(ang), jnp.sin(ang)
    rest = HEAD_C - rot
    c_tab = jnp.concatenate([cos, cos, jnp.ones((pos.shape[0], rest), F32)], axis=-1)
    s_tab = jnp.concatenate([-sin, sin, jnp.zeros((pos.shape[0], rest), F32)], axis=-1)
    return c_tab, s_tab


def _norm_rope(x, gain, c_tab, s_tab):
    half = HEAD_C // 8
    y = x * lax.rsqrt(jnp.mean(x * x, axis=-1, keepdims=True) + RMS_EPS) * gain
    swapped = jnp.concatenate([y[:, half:2 * half], y[:, 0:half], y[:, 2 * half:]], axis=-1)
    return y * c_tab + swapped * s_tab


def _softmax_sink(logits, valid, sink):
    logits = jnp.where(valid, logits, NEG_BIG)
    m = jnp.maximum(jnp.max(logits, axis=-1, keepdims=True), sink)
    p = jnp.where(valid, jnp.exp(logits - m), 0.0)
    denom = jnp.sum(p, axis=-1, keepdims=True) + jnp.exp(sink - m)
    return p / denom


def _qk_logits(q, k):
    return lax.dot_general(q.astype(BF16), k.astype(BF16), (((1,), (1,)), ((), ())),
                           preferred_element_type=F32) * (HEAD_C ** -0.5)


def _pv(p, v):
    return jnp.dot(p.astype(BF16), v.astype(BF16), preferred_element_type=F32)


def _attn_prompt_kernel(sinks_ref, q_ref, kv_ref, c_ref, s_ref, gq_ref, gk_ref, prev_ref, o_ref, newk_ref, newv_ref,
                        kprev_ref, vprev_ref, *, kv_heads, group):
    del prev_ref
    j = pl.program_id(1)
    kvw = kv_heads * HEAD_C

    @pl.when(j == 0)
    def _():
        kprev_ref[...] = jnp.zeros_like(kprev_ref)
        vprev_ref[...] = jnp.zeros_like(vprev_ref)

    c_tab = c_ref[...]
    s_tab = s_ref[...]
    rows = group * WINDOW
    row = lax.broadcasted_iota(jnp.int32, (rows, 2 * WINDOW), 0)
    cidx = lax.broadcasted_iota(jnp.int32, (rows, 2 * WINDOW), 1)
    r = lax.rem(row, WINDOW)
    first_key = WINDOW - jnp.minimum(j, 1) * WINDOW
    valid = (cidx > r) & (cidx <= r + WINDOW) & (cidx >= first_key)
    head_in_group = lax.broadcasted_iota(jnp.int32, (rows, 1), 0) // WINDOW

    for g in range(kv_heads):
        k_new = _norm_rope(kv_ref[:, g * HEAD_C:(g + 1) * HEAD_C], gk_ref[...], c_tab, s_tab)
        v_new = kv_ref[:, kvw + g * HEAD_C:kvw + (g + 1) * HEAD_C]
        keys = jnp.concatenate([kprev_ref[:, g * HEAD_C:(g + 1) * HEAD_C], k_new], axis=0)
        vals = jnp.concatenate([vprev_ref[:, g * HEAD_C:(g + 1) * HEAD_C], v_new], axis=0)
        heads = [g * group + i for i in range(group)]
        q_g = jnp.concatenate([_norm_rope(q_ref[:, h * HEAD_C:(h + 1) * HEAD_C], gq_ref[...], c_tab, s_tab)
                               for h in heads], axis=0)
        sink = jnp.zeros((rows, 1), F32)
        for i, h in enumerate(heads):
            sink = jnp.where(head_in_group == i, sinks_ref[h], sink)
        o_g = _pv(_softmax_sink(_qk_logits(q_g, keys), valid, sink), vals)
        for i, h in enumerate(heads):
            o_ref[:, h * HEAD_C:(h + 1) * HEAD_C] = o_g[i * WINDOW:(i + 1) * WINDOW, :]
        kprev_ref[:, g * HEAD_C:(g + 1) * HEAD_C] = k_new
        vprev_ref[:, g * HEAD_C:(g + 1) * HEAD_C] = v_new
    newk_ref[...] = kprev_ref[...]
    newv_ref[...] = vprev_ref[...]


def attn_prompt(proj, sinks, g_qn, g_kn, prev_out, *, batch, seq, q_col, k_col, q_heads, kv_heads):
    nb = seq // WINDOW
    qw = q_heads * HEAD_C
    kvw = kv_heads * HEAD_C
    c_tab, s_tab = _rope_tables(jnp.arange(seq))
    kern = functools.partial(_attn_prompt_kernel, kv_heads=kv_heads, group=q_heads // kv_heads)
    return pl.pallas_call(
        kern,
        grid=(batch, nb),
        in_specs=[pl.BlockSpec(memory_space=pltpu.SMEM),
                  pl.BlockSpec((WINDOW, qw), lambda b, j: (b * nb + j, q_col // qw)),
                  pl.BlockSpec((WINDOW, 2 * kvw), lambda b, j: (b * nb + j, k_col // (2 * kvw))),
                  pl.BlockSpec((WINDOW, HEAD_C), lambda b, j: (j, 0)),
                  pl.BlockSpec((WINDOW, HEAD_C), lambda b, j: (j, 0)),
                  pl.BlockSpec((1, HEAD_C), lambda b, j: (0, 0)),
                  pl.BlockSpec((1, HEAD_C), lambda b, j: (0, 0)),
                  pl.BlockSpec(memory_space=pl.ANY)],
        out_specs=[pl.BlockSpec((WINDOW, qw), lambda b, j: (b * nb + j, 0)),
                   pl.BlockSpec((None, WINDOW, kvw), lambda b, j: (b, 0, 0)),
                   pl.BlockSpec((None, WINDOW, kvw), lambda b, j: (b, 0, 0))],
        out_shape=[jax.ShapeDtypeStruct(prev_out.shape, F32),
                   jax.ShapeDtypeStruct((batch, WINDOW, kvw), F32),
                   jax.ShapeDtypeStruct((batch, WINDOW, kvw), F32)],
        scratch_shapes=[pltpu.VMEM((WINDOW, kvw), F32), pltpu.VMEM((WINDOW, kvw), F32)],
        input_output_aliases={7: 0},
        compiler_params=_params("parallel", "arbitrary"),
        name="attn_prompt",
    )(sinks, proj, proj, c_tab, s_tab, g_qn.reshape(1, HEAD_C), g_kn.reshape(1, HEAD_C), prev_out)


def _attn_sample_kernel(sinks_ref, q_ref, kv_ref, ck_ref, cv_ref, c_ref, s_ref, gq_ref, gk_ref, prev_ref,
                        o_ref, newk_ref, newv_ref, *, kv_heads, group, t_new):
    del prev_ref
    kvw = kv_heads * HEAD_C
    c_tab = c_ref[...]
    s_tab = s_ref[...]
    rows = group * t_new
    row = lax.broadcasted_iota(jnp.int32, (rows, 2 * WINDOW), 0)
    cidx = lax.broadcasted_iota(jnp.int32, (rows, 2 * WINDOW), 1)
    r = lax.rem(row, t_new)
    valid = ((cidx < WINDOW) & (cidx > r)) | ((cidx >= WINDOW) & (cidx - WINDOW <= r))
    head_in_group = lax.broadcasted_iota(jnp.int32, (rows, 1), 0) // t_new
    filler = jnp.zeros((WINDOW - t_new, HEAD_C), F32)

    for g in range(kv_heads):
        k_new = _norm_rope(kv_ref[:, g * HEAD_C:(g + 1) * HEAD_C], gk_ref[...], c_tab, s_tab)
        v_new = kv_ref[:, kvw + g * HEAD_C:kvw + (g + 1) * HEAD_C]
        k_old = ck_ref[:, g * HEAD_C:(g + 1) * HEAD_C]
        v_old = cv_ref[:, g * HEAD_C:(g + 1) * HEAD_C]
        keys = jnp.concatenate([k_old, k_new, filler], axis=0)
        vals = jnp.concatenate([v_old, v_new, filler], axis=0)
        heads = [g * group + i for i in range(group)]
        q_g = jnp.concatenate([_norm_rope(q_ref[:, h * HEAD_C:(h + 1) * HEAD_C], gq_ref[...], c_tab, s_tab)
                               for h in heads], axis=0)
        sink = jnp.zeros((rows, 1), F32)
        for i, h in enumerate(heads):
            sink = jnp.where(head_in_group == i, sinks_ref[h], sink)
        o_g = _pv(_softmax_sink(_qk_logits(q_g, keys), valid, sink), vals)
        for i, h in enumerate(heads):
            o_ref[:, h * HEAD_C:(h + 1) * HEAD_C] = o_g[i * t_new:(i + 1) * t_new, :]
        newk_ref[0:WINDOW - t_new, g * HEAD_C:(g + 1) * HEAD_C] = k_old[t_new:, :]
        newk_ref[WINDOW - t_new:, g * HEAD_C:(g + 1) * HEAD_C] = k_new
        newv_ref[0:WINDOW - t_new, g * HEAD_C:(g + 1) * HEAD_C] = v_old[t_new:, :]
        newv_ref[WINDOW - t_new:, g * HEAD_C:(g + 1) * HEAD_C] = v_new


def attn_sample(proj, cache_k, cache_v, sinks, g_qn, g_kn, prev_out, *, batch, seq, start, row0, q_col, k_col,
                q_heads, kv_heads):
    qw = q_heads * HEAD_C
    kvw = kv_heads * HEAD_C
    c_tab, s_tab = _rope_tables(start + jnp.arange(seq))
    kern = functools.partial(_attn_sample_kernel, kv_heads=kv_heads, group=q_heads // kv_heads, t_new=seq)
    blk0 = row0 // seq
    return pl.pallas_call(
        kern,
        grid=(batch,),
        in_specs=[pl.BlockSpec(memory_space=pltpu.SMEM),
                  pl.BlockSpec((seq, qw), lambda b: (blk0 + b, q_col // qw)),
                  pl.BlockSpec((seq, 2 * kvw), lambda b: (blk0 + b, k_col // (2 * kvw))),
                  pl.BlockSpec((None, WINDOW, kvw), lambda b: (b, 0, 0)),
                  pl.BlockSpec((None, WINDOW, kvw), lambda b: (b, 0, 0)),
                  pl.BlockSpec((seq, HEAD_C), lambda b: (0, 0)),
                  pl.BlockSpec((seq, HEAD_C), lambda b: (0, 0)),
                  pl.BlockSpec((1, HEAD_C), lambda b: (0, 0)),
                  pl.BlockSpec((1, HEAD_C), lambda b: (0, 0)),
                  pl.BlockSpec(memory_space=pl.ANY)],
        out_specs=[pl.BlockSpec((seq, qw), lambda b: (blk0 + b, 0)),
                   pl.BlockSpec((None, WINDOW, kvw), lambda b: (b, 0, 0)),
                   pl.BlockSpec((None, WINDOW, kvw), lambda b: (b, 0, 0))],
        out_shape=[jax.ShapeDtypeStruct(prev_out.shape, F32),
                   jax.ShapeDtypeStruct((batch, WINDOW, kvw), F32),
                   jax.ShapeDtypeStruct((batch, WINDOW, kvw), F32)],
        input_output_aliases={9: 0},
        compiler_params=_params("parallel"),
        name="attn_sample",
    )(sinks, proj, proj, cache_k, cache_v, c_tab, s_tab, g_qn.reshape(1, HEAD_C), g_kn.reshape(1, HEAD_C),
      prev_out)


def _ffn_up_kernel(te_ref, ns_ref, ts_ref, x_ref, w1_ref, w3_ref, o_ref, w1b_ref, w3b_ref, *, sub, n_sub):
    n_valid = ns_ref[pl.program_id(0)]

    @pl.when(n_valid > 0)
    def _():
        w1b_ref[...] = w1_ref[...].astype(BF16)
        w3b_ref[...] = w3_ref[...].astype(BF16)

        def compute(s, carry):
            rows = pl.ds(pl.multiple_of(s * sub, sub), sub)
            x = x_ref[rows, :]
            a = jnp.dot(x, w1b_ref[...], preferred_element_type=F32)
            b = jnp.dot(x, w3b_ref[...], preferred_element_type=F32)
            o_ref[rows, :] = (_silu(a) * b).astype(o_ref.dtype)
            return carry

        def clear(s, carry):
            rows = pl.ds(pl.multiple_of(s * sub, sub), sub)
            o_ref[rows, :] = jnp.zeros((sub, o_ref.shape[1]), o_ref.dtype)
            return carry

        lax.fori_loop(0, n_valid, compute, 0)
        lax.fori_loop(n_valid, n_sub, clear, 0)

    @pl.when(n_valid == 0)
    def _():
        o_ref[...] = jnp.zeros_like(o_ref)


def ffn_up(x, w1, w3, tile_expert, tile_nsub, tile_src, *, tile, sub):
    m, k = x.shape
    f = w1.shape[-1]
    tf = _pick(f, (256, 128))
    nf = f // tf

    def col(i, j, ns):
        return jnp.where(ns[i] > 0, j, nf - 1)

    return pl.pallas_call(
        functools.partial(_ffn_up_kernel, sub=sub, n_sub=tile // sub),
        grid_spec=pltpu.PrefetchScalarGridSpec(
            num_scalar_prefetch=3,
            grid=(m // tile, nf),
            in_specs=[pl.BlockSpec((tile, k), lambda i, j, te, ns, ts: (ts[i], 0), pipeline_mode=pl.Buffered(1)),
                      pl.BlockSpec((None, k, tf), lambda i, j, te, ns, ts: (te[i], 0, col(i, j, ns))),
                      pl.BlockSpec((None, k, tf), lambda i, j, te, ns, ts: (te[i], 0, col(i, j, ns)))],
            out_specs=pl.BlockSpec((tile, tf), lambda i, j, te, ns, ts: (i, j)),
            scratch_shapes=[pltpu.VMEM((k, tf), BF16), pltpu.VMEM((k, tf), BF16)]),
        out_shape=jax.ShapeDtypeStruct((m, f), BF16),
        compiler_params=_params("arbitrary", "arbitrary"),
        name="ffn_up",
    )(tile_expert, tile_nsub, tile_src, x, w1, w3)


def _ffn_down_kernel(te_ref, ns_ref, ts_ref, g_ref, w2_ref, *rest, sub, with_residual):
    if with_residual:
        r_ref, o_ref, wb_ref = rest
    else:
        o_ref, wb_ref = rest
    n_valid = ns_ref[pl.program_id(0)]

    @pl.when(n_valid > 0)
    def _():
        @pl.when(pl.program_id(2) == 0)
        def _():
            o_ref[...] = r_ref[...] if with_residual else jnp.zeros_like(o_ref)

        wb_ref[...] = w2_ref[...].astype(BF16)

        def compute(s, carry):
            rows = pl.ds(pl.multiple_of(s * sub, sub), sub)
            o_ref[rows, :] += jnp.dot(g_ref[rows, :], wb_ref[...], preferred_element_type=F32)
            return carry

        lax.fori_loop(0, n_valid, compute, 0)

    @pl.when((n_valid == 0) & (pl.program_id(2) == 0))
    def _():
        o_ref[...] = jnp.zeros_like(o_ref)


def ffn_down(g, w2, tile_expert, tile_nsub, tile_src, *, tile, sub, residual=None):
    m, f = g.shape
    n = w2.shape[-1]
    tn = _pick(n, (1024, 512, 256, 128))
    tk = _pick(f, (1024, 896, 512, 256, 128))
    n_j, n_k = n // tn, f // tk

    def col(i, j, ns):
        return jnp.where(ns[i] > 0, j, n_j - 1)

    def red(i, kk, ns):
        return jnp.where(ns[i] > 0, kk, n_k - 1)

    in_specs = [pl.BlockSpec((tile, tk), lambda i, j, kk, te, ns, ts: (ts[i], red(i, kk, ns))),
                pl.BlockSpec((None, tk, tn), lambda i, j, kk, te, ns, ts: (te[i], red(i, kk, ns), col(i, j, ns)))]
    args = [tile_expert, tile_nsub, tile_src, g, w2]
    if residual is not None:
        in_specs.append(pl.BlockSpec((tile, tn), lambda i, j, kk, te, ns, ts: (ts[i], col(i, j, ns)),
                                     pipeline_mode=pl.Buffered(1)))
        args.append(residual)
    return pl.pallas_call(
        functools.partial(_ffn_down_kernel, sub=sub, with_residual=residual is not None),
        grid_spec=pltpu.PrefetchScalarGridSpec(
            num_scalar_prefetch=3,
            grid=(m // tile, n_j, n_k),
            in_specs=in_specs,
            out_specs=pl.BlockSpec((tile, tn), lambda i, j, kk, te, ns, ts: (i, j)),
            scratch_shapes=[pltpu.VMEM((tk, tn), BF16)]),
        out_shape=jax.ShapeDtypeStruct((m, n), F32),
        compiler_params=_params("arbitrary", "arbitrary", "arbitrary"),
        name="ffn_down",
    )(*args)


def _router_kernel(x_ref, g_ref, w_ref, h_ref, idx_ref, gate_ref, *, n_experts):
    x = x_ref[...]
    h = x * lax.rsqrt(jnp.mean(x * x, axis=-1, keepdims=True) + RMS_EPS) * g_ref[...]
    h_ref[...] = h
    logits = jnp.dot(h, w_ref[...], precision=HIGHEST, preferred_element_type=F32)
    lane = lax.broadcasted_iota(jnp.int32, logits.shape, 1)
    logits = jnp.where(lane < n_experts, logits, -jnp.inf)
    v1 = jnp.max(logits, axis=-1, keepdims=True)
    i1 = jnp.min(jnp.where(logits == v1, lane, LANES), axis=-1, keepdims=True)
    rest = jnp.where(lane == i1, -jnp.inf, logits)
    v2 = jnp.max(rest, axis=-1, keepdims=True)
    i2 = jnp.min(jnp.where(rest == v2, lane, LANES), axis=-1, keepdims=True)
    e2 = jnp.exp(v2 - v1)
    g1 = 1.0 / (1.0 + e2)
    g2 = e2 / (1.0 + e2)
    idx_ref[...] = jnp.where(lane == 0, i1, jnp.where(lane == 1, i2, 0))
    gate_ref[...] = jnp.where(lane == 0, g1, jnp.where(lane == 1, g2, 0.0))


def router(x, g_ffn, w_router):
    m, d = x.shape
    n_experts = w_router.shape[-1]
    tm = _pick(m, (256, 128, 64, 32, 16))
    w_pad = jnp.zeros((d, LANES), F32).at[:, :n_experts].set(w_router)
    kern = functools.partial(_router_kernel, n_experts=n_experts)
    h, idx, gate = pl.pallas_call(
        kern,
        grid=(m // tm,),
        in_specs=[pl.BlockSpec((tm, d), lambda i: (i, 0)),
                  pl.BlockSpec((1, d), lambda i: (0, 0)),
                  pl.BlockSpec((d, LANES), lambda i: (0, 0))],
        out_specs=[pl.BlockSpec((tm, d), lambda i: (i, 0)),
                   pl.BlockSpec((tm, LANES), lambda i: (i, 0)),
                   pl.BlockSpec((tm, LANES), lambda i: (i, 0))],
        out_shape=[jax.ShapeDtypeStruct((m, d), F32),
                   jax.ShapeDtypeStruct((m, LANES), jnp.int32),
                   jax.ShapeDtypeStruct((m, LANES), F32)],
        compiler_params=_params("parallel"),
        name="router",
    )(x, g_ffn.reshape(1, d), w_pad)
    return h, idx[:, :TOP_K], gate[:, :TOP_K]


def _row_copy(src_ref, dst_ref, sem, src_row, dst_row):
    return pltpu.make_async_copy(src_ref.at[pl.ds(src_row, 1)], dst_ref.at[pl.ds(dst_row, 1)], sem)


def _dispatch_kernel(idx_ref, valid_ref, src_ref, o_ref, buf_ref, sem, *, rows):
    i = pl.program_id(0)

    @pl.when(valid_ref[i] > 0)
    def _():
        base = i * rows

        def start(r, carry):
            _row_copy(src_ref, buf_ref, sem, idx_ref[base + r], r).start()
            return carry

        def wait(r, carry):
            _row_copy(src_ref, buf_ref, sem, idx_ref[base + r], r).wait()
            return carry

        lax.fori_loop(0, rows, start, 0)
        lax.fori_loop(0, rows, wait, 0)
        o_ref[...] = buf_ref[...].astype(o_ref.dtype)

    @pl.when(valid_ref[i] == 0)
    def _():
        o_ref[...] = jnp.zeros_like(o_ref)


def dispatch_rows(src, idx, block_valid, rows):
    n = idx.shape[0]
    d = src.shape[1]
    return pl.pallas_call(
        functools.partial(_dispatch_kernel, rows=rows),
        grid_spec=pltpu.PrefetchScalarGridSpec(
            num_scalar_prefetch=2,
            grid=(n // rows,),
            in_specs=[pl.BlockSpec(memory_space=pl.ANY)],
            out_specs=pl.BlockSpec((rows, d), lambda i, idx, valid: (i, 0)),
            scratch_shapes=[pltpu.VMEM((rows, d), src.dtype), pltpu.SemaphoreType.DMA(())]),
        out_shape=jax.ShapeDtypeStruct((n, d), BF16),
        compiler_params=_params("arbitrary"),
        name="dispatch_rows",
    )(idx, block_valid, src)


def _combine_kernel(slot_ref, x_ref, gate_ref, ys_ref, o_ref, ya_ref, yb_ref, sems, *, rows):
    base = pl.program_id(0) * rows

    def copies(r):
        token = base + r
        return (_row_copy(ys_ref, ya_ref, sems.at[0], slot_ref[TOP_K * token], r),
                _row_copy(ys_ref, yb_ref, sems.at[1], slot_ref[TOP_K * token + 1], r))

    def start(r, carry):
        for cp in copies(r):
            cp.start()
        return carry

    def wait(r, carry):
        for cp in copies(r):
            cp.wait()
        return carry

    lax.fori_loop(0, rows, start, 0)
    lax.fori_loop(0, rows, wait, 0)
    gate = gate_ref[...]
    o_ref[...] = x_ref[...] + (gate[:, 0:1] * ya_ref[...] + gate[:, 1:2] * yb_ref[...])


def moe_combine(x, ys, slot, gate):
    m, d = x.shape
    rows = _pick(m, (256, 128, 64, 32, 16, 8))
    gate_pad = jnp.zeros((m, LANES), F32).at[:, :TOP_K].set(gate)
    return pl.pallas_call(
        functools.partial(_combine_kernel, rows=rows),
        grid_spec=pltpu.PrefetchScalarGridSpec(
            num_scalar_prefetch=1,
            grid=(m // rows,),
            in_specs=[pl.BlockSpec((rows, d), lambda i, slot: (i, 0)),
                      pl.BlockSpec((rows, LANES), lambda i, slot: (i, 0)),
                      pl.BlockSpec(memory_space=pl.ANY)],
            out_specs=pl.BlockSpec((rows, d), lambda i, slot: (i, 0)),
            scratch_shapes=[pltpu.VMEM((rows, d), F32), pltpu.VMEM((rows, d), F32),
                            pltpu.SemaphoreType.DMA((TOP_K,))]),
        out_shape=jax.ShapeDtypeStruct((m, d), F32),
        compiler_params=_params("arbitrary"),
        name="moe_combine",
    )(slot, x, gate_pad, ys)


MOE_SUB_ROWS = 256
MOE_TILE_HEADROOM = 1.05


def moe_ffn(x, g_ffn, w_router, w1, w3, w2, which):
    m, d = x.shape
    n_experts = w_router.shape[-1]
    w1, w3, w2 = (w.reshape((-1,) + w.shape[2:]) for w in (w1, w3, w2))
    h, top_i, top_w = router(x, g_ffn, w_router)

    n_assign = m * TOP_K
    sub = MOE_SUB_ROWS if n_assign >= 32 * MOE_SUB_ROWS else 32
    n_sub = -(-int(MOE_TILE_HEADROOM * n_assign / n_experts) // sub)
    tile = n_sub * sub
    n_tiles = (n_assign + n_experts * (tile - 1)) // tile
    flat_e = top_i.reshape(-1)
    counts = jnp.sum(jax.nn.one_hot(flat_e, n_experts, dtype=jnp.int32), axis=0)
    subs_per = (counts + sub - 1) // sub
    tiles_per = (subs_per + n_sub - 1) // n_sub
    tile_end = jnp.cumsum(tiles_per)
    tile_begin = tile_end - tiles_per
    order = jnp.argsort(flat_e, stable=True)
    sorted_e = flat_e[order]
    rank = jnp.arange(n_assign, dtype=jnp.int32) - (jnp.cumsum(counts) - counts)[sorted_e]
    slot_sorted = tile_begin[sorted_e] * tile + rank
    slot = jnp.zeros((n_assign,), jnp.int32).at[order].set(slot_sorted)
    row_token = jnp.zeros((n_tiles * tile,), jnp.int32).at[slot_sorted].set((order // TOP_K).astype(jnp.int32))
    tile_ids = jnp.arange(n_tiles, dtype=jnp.int32)
    n_used = tile_end[-1]
    owner = jnp.minimum(jnp.searchsorted(tile_end, tile_ids, side="right"), n_experts - 1).astype(jnp.int32)
    tile_nsub = jnp.clip(subs_per[owner] - (tile_ids - tile_begin[owner]) * n_sub, 0, n_sub)
    tile_nsub = jnp.where(tile_ids < n_used, tile_nsub, 0).astype(jnp.int32)
    tile_src = jnp.minimum(tile_ids, n_used - 1).astype(jnp.int32)
    tile_expert = owner[tile_src] + which * n_experts
    sub_valid = (jnp.arange(n_sub, dtype=jnp.int32)[None, :] < tile_nsub[:, None]).astype(jnp.int32).reshape(-1)

    xs = dispatch_rows(h, row_token, sub_valid, sub)
    g = ffn_up(xs, w1, w3, tile_expert, tile_nsub, tile_src, tile=tile, sub=sub)
    ys = ffn_down(g, w2, tile_expert, tile_nsub, tile_src, tile=tile, sub=sub)
    return moe_combine(x, ys, slot, top_w)


def dense_ffn(x, g_ffn, w1, w3, w2, which):
    m = x.shape[0]
    h = rmsnorm_rows(x, g_ffn, BF16)
    tile = _pick(m, (2112, 2048, 1024, 528, 512, 256, 128, 64, 32, 16))
    sub = _pick(tile, (528, 512, 256, 128, 64, 32, 16))
    n_tiles = m // tile
    tile_expert = jnp.full((n_tiles,), which, jnp.int32)
    tile_nsub = jnp.full((n_tiles,), tile // sub, jnp.int32)
    tile_src = jnp.arange(n_tiles, dtype=jnp.int32)
    g = ffn_up(h, w1, w3, tile_expert, tile_nsub, tile_src, tile=tile, sub=sub)
    return ffn_down(g, w2, tile_expert, tile_nsub, tile_src, tile=tile, sub=sub, residual=x)


def kernel(x_prompt, x_sample, state_hgrn, state_conv, cache_k, cache_v, lb_param, g_mix, w_in, g_hgrn, w_out_a, w_dw, b_dw, ln_g, ln_b, w_out_b, g_qn, g_kn, sinks, w_out_c, w_o, g_ffn, w1_dense, w3_dense, w2_dense, w_router, w1_moe, w3_moe, w2_moe):
    bp, tp, d = x_prompt.shape
    bs, ts, _ = x_sample.shape
    depth = w_in.shape[0]
    heads_a = state_hgrn.shape[2]
    channels_b = state_conv.shape[-1]
    conv_w = w_dw.shape[1]
    kv_heads = cache_k.shape[3]
    q_heads = sinks.shape[1]
    n_prompt = bp * tp
    m_rows = n_prompt + bs * ts
    past_len = PAST_LEN

    a_cols = heads_a * HEAD_A
    lin_col = 4 * a_cols
    gate_b_col = lin_col + channels_b
    q_col = gate_b_col + channels_b
    k_col = q_col + q_heads * HEAD_C
    merge_gate_col = k_col + 2 * kv_heads * HEAD_C

    p_soft = jax.nn.softmax(lb_param.astype(F32), axis=0)
    lower = jnp.cumsum(p_soft, axis=0) - p_soft[0:1]

    x = jnp.concatenate([x_prompt.reshape(n_prompt, d), x_sample.reshape(bs * ts, d)], axis=0)
    zero_state = jnp.zeros((bp, heads_a, HEAD_A, HEAD_A), F32)
    zero_hist = jnp.zeros((bp, HIST_ROWS, channels_b), F32)
    outs = {name: [] for name in ("hp", "cp", "kp", "vp", "hs", "cs", "ks", "vs")}

    for l in range(depth):
        h = rmsnorm_rows(x, g_mix[l], BF16)
        proj = gemm(h, w_in, l)

        o_a, s_p = hgrn_mixer(proj, lower[l], g_hgrn[l], zero_state, batch=bp, seq=tp,
                              chunk=_pick(tp, (128, 64, 32, 16, 8)), row0=0, heads=heads_a, total_rows=m_rows,
                              prev_out=jnp.zeros((m_rows, a_cols), F32))
        o_a, s_s = hgrn_mixer(proj, lower[l], g_hgrn[l], state_hgrn[l], batch=bs, seq=ts,
                              chunk=_pick(ts, (128, 64, 32, 16, 8)), row0=n_prompt, heads=heads_a, total_rows=m_rows,
                              prev_out=o_a)

        hist_s = jnp.pad(state_conv[l], ((0, 0), (HIST_ROWS - (conv_w - 1), 0), (0, 0)))
        conv, c_p = conv_mixer(proj, zero_hist, w_dw[l], b_dw[l], batch=bp, seq=tp,
                               tb=_pick(tp, (128, 64, 32, 16, 8)), row0=0, lin_col=lin_col, gate_col=gate_b_col,
                               total_rows=m_rows, prev_out=jnp.zeros((m_rows, channels_b), F32))
        conv, c_s = conv_mixer(proj, hist_s, w_dw[l], b_dw[l], batch=bs, seq=ts,
                               tb=_pick(ts, (128, 64, 32, 16, 8)), row0=n_prompt, lin_col=lin_col,
                               gate_col=gate_b_col, total_rows=m_rows, prev_out=conv)
        o_b = layernorm_silu_rows(conv, ln_g[l], ln_b[l], BF16)

        o_c, k_p, v_p = attn_prompt(proj, sinks[l], g_qn[l], g_kn[l], jnp.zeros((m_rows, q_heads * HEAD_C), F32),
                                    batch=bp, seq=tp, q_col=q_col, k_col=k_col, q_heads=q_heads, kv_heads=kv_heads)
        o_c, k_s, v_s = attn_sample(proj, cache_k[l].reshape(bs, WINDOW, kv_heads * HEAD_C),
                                    cache_v[l].reshape(bs, WINDOW, kv_heads * HEAD_C), sinks[l], g_qn[l], g_kn[l],
                                    o_c, batch=bs, seq=ts, start=past_len, row0=n_prompt, q_col=q_col, k_col=k_col,
                                    q_heads=q_heads, kv_heads=kv_heads)

        mixed = gated_merge(o_a, o_b, o_c, w_out_a, w_out_b, w_out_c, proj, l, merge_gate_col, d)
        x = gemm(mixed, w_o, l, residual=x)

        j = l // 2
        if l % 2 == 0:
            x = dense_ffn(x, g_ffn[l], w1_dense, w3_dense, w2_dense, j)
        else:
            x = moe_ffn(x, g_ffn[l], w_router[j], w1_moe, w3_moe, w2_moe, j)

        outs["hp"].append(s_p)
        outs["cp"].append(c_p)
        outs["kp"].append(k_p.reshape(bp, WINDOW, kv_heads, HEAD_C))
        outs["vp"].append(v_p.reshape(bp, WINDOW, kv_heads, HEAD_C))
        outs["hs"].append(s_s)
        outs["cs"].append(c_s)
        outs["ks"].append(k_s.reshape(bs, WINDOW, kv_heads, HEAD_C))
        outs["vs"].append(v_s.reshape(bs, WINDOW, kv_heads, HEAD_C))

    y_prompt = x[:n_prompt].reshape(bp, tp, d)
    y_sample = x[n_prompt:].reshape(bs, ts, d)
    return (y_prompt, y_sample,
            jnp.stack(outs["hp"]), jnp.stack(outs["cp"]), jnp.stack(outs["kp"]), jnp.stack(outs["vp"]),
            jnp.stack(outs["hs"]), jnp.stack(outs["cs"]), jnp.stack(outs["ks"]), jnp.stack(outs["vs"]))
```

```python
import functools

import jax
import jax.numpy as jnp
from jax import lax
from jax.experimental import pallas as pl
from jax.experimental.pallas import tpu as pltpu

F32 = jnp.float32
BF16 = jnp.bfloat16

RMS_EPS = 1e-6
LN_EPS = 1e-5
F_MIN = 1e-30
NEG_BIG = -1e30
ROPE_THETA = 500000.0
HEAD_A = 128
HEAD_C = 64
WINDOW = 128
PAST_LEN = 16384
TOP_K = 2
HIGHEST = lax.Precision.HIGHEST

V7X_VMEM_LIMIT_BYTES = 56 * 1024 * 1024
LANES = 128
SUBLANES = 8


def _pick(n, candidates):
    for c in candidates:
        if n % c == 0:
            return c
    raise ValueError(f"no tile in {candidates} divides {n}")


def _params(*semantics):
    return pltpu.CompilerParams(dimension_semantics=semantics, vmem_limit_bytes=V7X_VMEM_LIMIT_BYTES)


def _silu(x):
    return x * jax.nn.sigmoid(x)


def _rmsnorm_kernel(x_ref, g_ref, o_ref):
    x = x_ref[...]
    ms = jnp.mean(x * x, axis=-1, keepdims=True)
    o_ref[...] = (x * lax.rsqrt(ms + RMS_EPS) * g_ref[...]).astype(o_ref.dtype)


def rmsnorm_rows(x, g, out_dtype):
    m, d = x.shape
    tm = _pick(m, (256, 128, 64, 32, 16))
    return pl.pallas_call(
        _rmsnorm_kernel,
        grid=(m // tm,),
        in_specs=[pl.BlockSpec((tm, d), lambda i: (i, 0)), pl.BlockSpec((1, d), lambda i: (0, 0))],
        out_specs=pl.BlockSpec((tm, d), lambda i: (i, 0)),
        out_shape=jax.ShapeDtypeStruct((m, d), out_dtype),
        compiler_params=_params("parallel"),
        name="rmsnorm_rows",
    )(x, g.reshape(1, d))


def _ln_silu_kernel(x_ref, g_ref, b_ref, o_ref):
    x = x_ref[...]
    mu = jnp.mean(x, axis=-1, keepdims=True)
    xc = x - mu
    var = jnp.mean(xc * xc, axis=-1, keepdims=True)
    y = xc * lax.rsqrt(var + LN_EPS) * g_ref[...] + b_ref[...]
    o_ref[...] = _silu(y).astype(o_ref.dtype)


def layernorm_silu_rows(x, g, b, out_dtype):
    m, d = x.shape
    tm = _pick(m, (256, 128, 64, 32, 16))
    return pl.pallas_call(
        _ln_silu_kernel,
        grid=(m // tm,),
        in_specs=[pl.BlockSpec((tm, d), lambda i: (i, 0)),
                  pl.BlockSpec((1, d), lambda i: (0, 0)),
                  pl.BlockSpec((1, d), lambda i: (0, 0))],
        out_specs=pl.BlockSpec((tm, d), lambda i: (i, 0)),
        out_shape=jax.ShapeDtypeStruct((m, d), out_dtype),
        compiler_params=_params("parallel"),
        name="layernorm_silu_rows",
    )(x, g.reshape(1, d), b.reshape(1, d))


def _gemm_kernel(x_ref, w_ref, o_ref):
    o_ref[...] = jnp.dot(x_ref[...].astype(BF16), w_ref[...].astype(BF16), preferred_element_type=F32)


def _gemm_residual_kernel(x_ref, w_ref, r_ref, o_ref):
    o_ref[...] = r_ref[...] + jnp.dot(x_ref[...].astype(BF16), w_ref[...].astype(BF16),
                                      preferred_element_type=F32)


def gemm(x, w, layer, residual=None):
    m, k = x.shape
    n = w.shape[-1]
    tm = _pick(m, (1056, 1024, 512, 256, 128, 64, 32, 16))
    tn = _pick(n, (512, 256, 128))
    in_specs = [pl.BlockSpec((tm, k), lambda i, j: (i, 0)),
                pl.BlockSpec((None, k, tn), lambda i, j: (layer, 0, j))]
    args = [x, w]
    body = _gemm_kernel
    if residual is not None:
        in_specs.append(pl.BlockSpec((tm, tn), lambda i, j: (i, j)))
        args.append(residual)
        body = _gemm_residual_kernel
    return pl.pallas_call(
        body,
        grid=(m // tm, n // tn),
        in_specs=in_specs,
        out_specs=pl.BlockSpec((tm, tn), lambda i, j: (i, j)),
        out_shape=jax.ShapeDtypeStruct((m, n), F32),
        compiler_params=_params("parallel", "arbitrary"),
        name="gemm_residual" if residual is not None else "gemm",
    )(*args)


def _merge_kernel(oa_ref, ob_ref, oc_ref, wa_ref, wb_ref, wc_ref, ga_ref, gb_ref, gc_ref, o_ref):
    def branch(o_ref_, w_ref_, g_ref_):
        y = jnp.dot(o_ref_[...].astype(BF16), w_ref_[...].astype(BF16), preferred_element_type=F32)
        return jax.nn.sigmoid(g_ref_[...]) * y

    mixed = branch(oa_ref, wa_ref, ga_ref) + branch(ob_ref, wb_ref, gb_ref) + branch(oc_ref, wc_ref, gc_ref)
    o_ref[...] = mixed.astype(o_ref.dtype)


def gated_merge(o_a, o_b, o_c, w_out_a, w_out_b, w_out_c, proj, layer, gate_col, d_model):
    m = o_a.shape[0]
    tm = _pick(m, (1056, 1024, 512, 256, 128, 64, 32, 16))
    tn = next(c for c in (256, 128) if d_model % c == 0 and gate_col % c == 0)
    gate_blk = gate_col // tn
    per_gate = d_model // tn

    def o_spec(o):
        return pl.BlockSpec((tm, o.shape[1]), lambda i, j: (i, 0), pipeline_mode=pl.Buffered(1))

    def w_spec(w):
        return pl.BlockSpec((None, w.shape[1], tn), lambda i, j: (layer, 0, j))

    def g_spec(which):
        return pl.BlockSpec((tm, tn), lambda i, j: (i, gate_blk + which * per_gate + j))

    return pl.pallas_call(
        _merge_kernel,
        grid=(m // tm, d_model // tn),
        in_specs=[o_spec(o_a), o_spec(o_b), o_spec(o_c), w_spec(w_out_a), w_spec(w_out_b), w_spec(w_out_c),
                  g_spec(0), g_spec(1), g_spec(2)],
        out_specs=pl.BlockSpec((tm, tn), lambda i, j: (i, j)),
        out_shape=jax.ShapeDtypeStruct((m, d_model), BF16),
        compiler_params=_params("parallel", "arbitrary"),
        name="gated_merge",
    )(o_a, o_b, o_c, w_out_a, w_out_b, w_out_c, proj, proj, proj)


def _cumsum_rows(x):
    n = x.shape[0]
    row = lax.broadcasted_iota(jnp.int32, x.shape, 0)
    shift = 1
    while shift < n:
        x = x + jnp.where(row >= shift, pltpu.roll(x, shift, axis=0), 0.0)
        shift *= 2
    return x


def _bf16_dot_t(a, b):
    return lax.dot_general(a.astype(BF16), b.astype(BF16), (((1,), (1,)), ((), ())), preferred_element_type=F32)


def _hgrn_head(q_in, f_in, v, gate_in, lb, gain, st, *, chunk):
    sig = jax.nn.sigmoid(f_in)
    f_gate = lb + (1.0 - lb) * sig
    log_f = jnp.log(jnp.maximum(f_gate, F_MIN))
    k = (1.0 - lb) * (1.0 - sig)
    q = _silu(q_in)
    b = _cumsum_rows(log_f)

    o = _bf16_dot_t(q * jnp.exp(b), st)

    n_sub = chunk // SUBLANES
    q3, k3, b3, v3 = (a.reshape(n_sub, SUBLANES, HEAD_A) for a in (q, k, b, v))
    r8 = lax.broadcasted_iota(jnp.int32, (n_sub, SUBLANES, 1), 1)
    acc = jnp.zeros((n_sub, SUBLANES, HEAD_A), F32)
    for s8 in range(SUBLANES):
        causal = r8 >= s8
        decay = jnp.exp(jnp.where(causal, b3 - b3[:, s8:s8 + 1, :], 0.0))
        score = jnp.sum(jnp.where(causal, q3 * k3[:, s8:s8 + 1, :] * decay, 0.0), axis=-1, keepdims=True)
        acc = acc + score * v3[:, s8:s8 + 1, :]
    o = o + acc.reshape(chunk, HEAD_A)

    row = lax.broadcasted_iota(jnp.int32, (chunk, 1), 0)
    scores = None
    m = SUBLANES
    while 2 * m <= chunk:
        n_blk = chunk // (2 * m)
        ref_row = b.reshape(n_blk, 2 * m, HEAD_A)[:, m - 1:m, :]
        beta = jnp.broadcast_to(ref_row, (n_blk, 2 * m, HEAD_A)).reshape(chunk, HEAD_A)
        second = lax.rem(row, 2 * m) >= m
        q_m = jnp.where(second, q * jnp.exp(jnp.where(second, b - beta, 0.0)), 0.0)
        k_m = jnp.where(second, 0.0, k * jnp.exp(jnp.where(second, 0.0, beta - b)))
        a = _bf16_dot_t(q_m, k_m)
        if n_blk > 1:
            blk_r = lax.broadcasted_iota(jnp.int32, (chunk, chunk), 0) // (2 * m)
            blk_c = lax.broadcasted_iota(jnp.int32, (chunk, chunk), 1) // (2 * m)
            a = jnp.where(blk_r == blk_c, a, 0.0)
        scores = a if scores is None else scores + a
        m *= 2
    if scores is not None:
        o = o + jnp.dot(scores.astype(BF16), v.astype(BF16), preferred_element_type=F32)

    o_n = o * lax.rsqrt(jnp.mean(o * o, axis=-1, keepdims=True) + RMS_EPS) * gain
    out = o_n * _silu(gate_in)

    b_last = b[chunk - 1:chunk, :]
    k_dec = k * jnp.exp(b_last - b)
    st_new = jnp.exp(b_last) * st + lax.dot_general(v.astype(BF16), k_dec.astype(BF16), (((0,), (0,)), ((), ())),
                                                    preferred_element_type=F32)
    return out, st_new


def _hgrn_kernel(q_ref, f_ref, i_ref, g_ref, lb_ref, gh_ref, s0_ref, *rest, chunk, n_chunks, heads_per_step,
                 aliased):
    o_ref, sfin_ref, st_ref = rest[1:] if aliased else rest
    c = pl.program_id(2)

    @pl.when(c == 0)
    def _():
        for hh in range(heads_per_step):
            st_ref[hh] = s0_ref[hh].T

    for hh in range(heads_per_step):
        cols = slice(hh * HEAD_A, (hh + 1) * HEAD_A)
        out, st_new = _hgrn_head(q_ref[:, cols], f_ref[:, cols], i_ref[:, cols], g_ref[:, cols], lb_ref[:, cols],
                                 gh_ref[...], st_ref[hh], chunk=chunk)
        o_ref[:, cols] = out
        st_ref[hh] = st_new

        @pl.when(c == n_chunks - 1)
        def _(hh=hh, st_new=st_new):
            sfin_ref[hh] = st_new.T


def hgrn_mixer(proj, lb, g_hgrn, s0, *, batch, seq, chunk, row0, heads, total_rows, prev_out=None):
    n_chunks = seq // chunk
    blk0 = row0 // chunk
    hps = _pick(heads, (4, 2, 1))
    width = hps * HEAD_A

    def p_spec(slab):
        return pl.BlockSpec((chunk, width), lambda b, h, c: (blk0 + b * n_chunks + c, slab * (heads // hps) + h))

    in_specs = [p_spec(0), p_spec(1), p_spec(2), p_spec(3),
                pl.BlockSpec((1, width), lambda b, h, c: (0, h)),
                pl.BlockSpec((1, HEAD_A), lambda b, h, c: (0, 0)),
                pl.BlockSpec((None, hps, HEAD_A, HEAD_A), lambda b, h, c: (b, h, 0, 0))]
    args = [proj, proj, proj, proj, lb.reshape(1, heads * HEAD_A), g_hgrn.reshape(1, HEAD_A), s0]
    aliases = {}
    if prev_out is not None:
        in_specs.append(pl.BlockSpec(memory_space=pl.ANY))
        args.append(prev_out)
        aliases = {len(args) - 1: 0}
    kern = functools.partial(_hgrn_kernel, chunk=chunk, n_chunks=n_chunks, heads_per_step=hps,
                             aliased=prev_out is not None)
    return pl.pallas_call(
        kern,
        grid=(batch, heads // hps, n_chunks),
        in_specs=in_specs,
        out_specs=[pl.BlockSpec((chunk, width), lambda b, h, c: (blk0 + b * n_chunks + c, h)),
                   pl.BlockSpec((None, hps, HEAD_A, HEAD_A), lambda b, h, c: (b, h, 0, 0))],
        out_shape=[jax.ShapeDtypeStruct((total_rows, heads * HEAD_A), F32),
                   jax.ShapeDtypeStruct((batch, heads, HEAD_A, HEAD_A), F32)],
        scratch_shapes=[pltpu.VMEM((hps, HEAD_A, HEAD_A), F32)],
        input_output_aliases=aliases,
        compiler_params=_params("parallel", "parallel", "arbitrary"),
        name="hgrn_mixer",
    )(*args)


HIST_ROWS = 32


def _conv_kernel(ul_ref, ug_ref, hist_ref, w_ref, bias_ref, *rest, tb, conv_w, aliased):
    o_ref, state_ref, buf_ref = rest[1:] if aliased else rest
    t = pl.program_id(2)

    @pl.when(t == 0)
    def _():
        buf_ref[0:HIST_ROWS, :] = hist_ref[...]

    u = ul_ref[...] * jax.nn.sigmoid(ug_ref[...])
    buf_ref[HIST_ROWS:HIST_ROWS + tb, :] = u
    first = HIST_ROWS - (conv_w - 1)
    acc = jnp.zeros(u.shape, F32)
    for j in range(conv_w):
        acc = acc + w_ref[j:j + 1, :] * buf_ref[first + j:first + j + tb, :]
    o_ref[...] = acc + bias_ref[...]
    state_ref[...] = buf_ref[tb + first:tb + HIST_ROWS, :]
    tail = buf_ref[tb:tb + HIST_ROWS, :]
    buf_ref[0:HIST_ROWS, :] = tail


def conv_mixer(proj, hist, w_dw, b_dw, *, batch, seq, tb, row0, lin_col, gate_col, total_rows, prev_out=None):
    conv_w, channels = w_dw.shape
    cw = _pick(channels, (256, 128))
    n_t = seq // tb
    blk0 = row0 // tb
    in_specs = [pl.BlockSpec((tb, cw), lambda b, c, t: (blk0 + b * n_t + t, lin_col // cw + c)),
                pl.BlockSpec((tb, cw), lambda b, c, t: (blk0 + b * n_t + t, gate_col // cw + c)),
                pl.BlockSpec((None, HIST_ROWS, cw), lambda b, c, t: (b, 0, c)),
                pl.BlockSpec((conv_w, cw), lambda b, c, t: (0, c)),
                pl.BlockSpec((1, cw), lambda b, c, t: (0, c))]
    args = [proj, proj, hist, w_dw, b_dw.reshape(1, channels)]
    aliases = {}
    if prev_out is not None:
        in_specs.append(pl.BlockSpec(memory_space=pl.ANY))
        args.append(prev_out)
        aliases = {len(args) - 1: 0}
    kern = functools.partial(_conv_kernel, tb=tb, conv_w=conv_w, aliased=prev_out is not None)
    return pl.pallas_call(
        kern,
        grid=(batch, channels // cw, n_t),
        in_specs=in_specs,
        out_specs=[pl.BlockSpec((tb, cw), lambda b, c, t: (blk0 + b * n_t + t, c)),
                   pl.BlockSpec((None, conv_w - 1, cw), lambda b, c, t: (b, 0, c))],
        out_shape=[jax.ShapeDtypeStruct((total_rows, channels), F32),
                   jax.ShapeDtypeStruct((batch, conv_w - 1, channels), F32)],
        scratch_shapes=[pltpu.VMEM((HIST_ROWS + tb, cw), F32)],
        input_output_aliases=aliases,
        compiler_params=_params("parallel", "parallel", "arbitrary"),
        name="conv_mixer",
    )(*args)


def _rope_tables(pos):
    rot = HEAD_C // 4
    inv_freq = jnp.float32(ROPE_THETA) ** (-jnp.arange(0, rot, 2, dtype=F32) / rot)
    ang = pos.astype(F32)[:, None] * inv_freq[None, :]
    cos, sin = jnp.cos(ang), jnp.sin(ang)
    rest = HEAD_C - rot
    c_tab = jnp.concatenate([cos, cos, jnp.ones((pos.shape[0], rest), F32)], axis=-1)
    s_tab = jnp.concatenate([-sin, sin, jnp.zeros((pos.shape[0], rest), F32)], axis=-1)
    return c_tab, s_tab


def _norm_rope(x, gain, c_tab, s_tab):
    half = HEAD_C // 8
    y = x * lax.rsqrt(jnp.mean(x * x, axis=-1, keepdims=True) + RMS_EPS) * gain
    swapped = jnp.concatenate([y[:, half:2 * half], y[:, 0:half], y[:, 2 * half:]], axis=-1)
    return y * c_tab + swapped * s_tab


def _softmax_sink(logits, valid, sink):
    logits = jnp.where(valid, logits, NEG_BIG)
    m = jnp.maximum(jnp.max(logits, axis=-1, keepdims=True), sink)
    p = jnp.where(valid, jnp.exp(logits - m), 0.0)
    denom = jnp.sum(p, axis=-1, keepdims=True) + jnp.exp(sink - m)
    return p / denom


def _qk_logits(q, k):
    return lax.dot_general(q.astype(BF16), k.astype(BF16), (((1,), (1,)), ((), ())),
                           preferred_element_type=F32) * (HEAD_C ** -0.5)


def _pv(p, v):
    return jnp.dot(p.astype(BF16), v.astype(BF16), preferred_element_type=F32)


def _attn_prompt_kernel(sinks_ref, q_ref, kv_ref, c_ref, s_ref, gq_ref, gk_ref, prev_ref, o_ref, newk_ref, newv_ref,
                        kprev_ref, vprev_ref, *, kv_heads, group):
    del prev_ref
    j = pl.program_id(1)
    kvw = kv_heads * HEAD_C

    @pl.when(j == 0)
    def _():
        kprev_ref[...] = jnp.zeros_like(kprev_ref)
        vprev_ref[...] = jnp.zeros_like(vprev_ref)

    c_tab = c_ref[...]
    s_tab = s_ref[...]
    rows = group * WINDOW
    row = lax.broadcasted_iota(jnp.int32, (rows, 2 * WINDOW), 0)
    cidx = lax.broadcasted_iota(jnp.int32, (rows, 2 * WINDOW), 1)
    r = lax.rem(row, WINDOW)
    first_key = WINDOW - jnp.minimum(j, 1) * WINDOW
    valid = (cidx > r) & (cidx <= r + WINDOW) & (cidx >= first_key)
    head_in_group = lax.broadcasted_iota(jnp.int32, (rows, 1), 0) // WINDOW

    gain_q = gq_ref[...]
    gain_k = gk_ref[...]
    for g in range(kv_heads):
        k_new = _norm_rope(kv_ref[:, g * HEAD_C:(g + 1) * HEAD_C], gain_k, c_tab, s_tab)
        v_new = kv_ref[:, kvw + g * HEAD_C:kvw + (g + 1) * HEAD_C]
        keys = jnp.concatenate([kprev_ref[g], k_new], axis=0)
        vals = jnp.concatenate([vprev_ref[g], v_new], axis=0)
        heads = [g * group + i for i in range(group)]
        q_g = jnp.concatenate([_norm_rope(q_ref[:, h * HEAD_C:(h + 1) * HEAD_C], gain_q, c_tab, s_tab)
                               for h in heads], axis=0)
        sink = jnp.zeros((rows, 1), F32)
        for i, h in enumerate(heads):
            sink = jnp.where(head_in_group == i, sinks_ref[h], sink)
        o_g = _pv(_softmax_sink(_qk_logits(q_g, keys), valid, sink), vals)
        for i, h in enumerate(heads):
            o_ref[:, h * HEAD_C:(h + 1) * HEAD_C] = o_g[i * WINDOW:(i + 1) * WINDOW, :]
        kprev_ref[g] = k_new
        vprev_ref[g] = v_new
        newk_ref[:, g * HEAD_C:(g + 1) * HEAD_C] = k_new
        newv_ref[:, g * HEAD_C:(g + 1) * HEAD_C] = v_new


def attn_prompt(proj, sinks, g_qn, g_kn, prev_out, *, batch, seq, q_col, k_col, q_heads, kv_heads):
    nb = seq // WINDOW
    qw = q_heads * HEAD_C
    kvw = kv_heads * HEAD_C
    c_tab, s_tab = _rope_tables(jnp.arange(seq))
    kern = functools.partial(_attn_prompt_kernel, kv_heads=kv_heads, group=q_heads // kv_heads)
    return pl.pallas_call(
        kern,
        grid=(batch, nb),
        in_specs=[pl.BlockSpec(memory_space=pltpu.SMEM),
                  pl.BlockSpec((WINDOW, qw), lambda b, j: (b * nb + j, q_col // qw)),
                  pl.BlockSpec((WINDOW, 2 * kvw), lambda b, j: (b * nb + j, k_col // (2 * kvw))),
                  pl.BlockSpec((WINDOW, HEAD_C), lambda b, j: (j, 0)),
                  pl.BlockSpec((WINDOW, HEAD_C), lambda b, j: (j, 0)),
                  pl.BlockSpec((1, HEAD_C), lambda b, j: (0, 0)),
                  pl.BlockSpec((1, HEAD_C), lambda b, j: (0, 0)),
                  pl.BlockSpec(memory_space=pl.ANY)],
        out_specs=[pl.BlockSpec((WINDOW, qw), lambda b, j: (b * nb + j, 0)),
                   pl.BlockSpec((None, WINDOW, kvw), lambda b, j: (b, 0, 0)),
                   pl.BlockSpec((None, WINDOW, kvw), lambda b, j: (b, 0, 0))],
        out_shape=[jax.ShapeDtypeStruct(prev_out.shape, F32),
                   jax.ShapeDtypeStruct((batch, WINDOW, kvw), F32),
                   jax.ShapeDtypeStruct((batch, WINDOW, kvw), F32)],
        scratch_shapes=[pltpu.VMEM((kv_heads, WINDOW, HEAD_C), F32), pltpu.VMEM((kv_heads, WINDOW, HEAD_C), F32)],
        input_output_aliases={7: 0},
        compiler_params=_params("parallel", "arbitrary"),
        name="attn_prompt",
    )(sinks, proj, proj, c_tab, s_tab, g_qn.reshape(1, HEAD_C), g_kn.reshape(1, HEAD_C), prev_out)


def _attn_sample_kernel(sinks_ref, q_ref, kv_ref, ck_ref, cv_ref, c_ref, s_ref, gq_ref, gk_ref, prev_ref,
                        o_ref, newk_ref, newv_ref, *, kv_heads, group, t_new):
    del prev_ref
    kvw = kv_heads * HEAD_C
    c_tab = c_ref[...]
    s_tab = s_ref[...]
    rows = group * t_new
    row = lax.broadcasted_iota(jnp.int32, (rows, 2 * WINDOW), 0)
    cidx = lax.broadcasted_iota(jnp.int32, (rows, 2 * WINDOW), 1)
    r = lax.rem(row, t_new)
    valid = ((cidx < WINDOW) & (cidx > r)) | ((cidx >= WINDOW) & (cidx - WINDOW <= r))
    head_in_group = lax.broadcasted_iota(jnp.int32, (rows, 1), 0) // t_new
    filler = jnp.zeros((WINDOW - t_new, HEAD_C), F32)

    for g in range(kv_heads):
        k_new = _norm_rope(kv_ref[:, g * HEAD_C:(g + 1) * HEAD_C], gk_ref[...], c_tab, s_tab)
        v_new = kv_ref[:, kvw + g * HEAD_C:kvw + (g + 1) * HEAD_C]
        k_old = ck_ref[:, g * HEAD_C:(g + 1) * HEAD_C]
        v_old = cv_ref[:, g * HEAD_C:(g + 1) * HEAD_C]
        keys = jnp.concatenate([k_old, k_new, filler], axis=0)
        vals = jnp.concatenate([v_old, v_new, filler], axis=0)
        heads = [g * group + i for i in range(group)]
        q_g = jnp.concatenate([_norm_rope(q_ref[:, h * HEAD_C:(h + 1) * HEAD_C], gq_ref[...], c_tab, s_tab)
                               for h in heads], axis=0)
        sink = jnp.zeros((rows, 1), F32)
        for i, h in enumerate(heads):
            sink = jnp.where(head_in_group == i, sinks_ref[h], sink)
        o_g = _pv(_softmax_sink(_qk_logits(q_g, keys), valid, sink), vals)
        for i, h in enumerate(heads):
            o_ref[:, h * HEAD_C:(h + 1) * HEAD_C] = o_g[i * t_new:(i + 1) * t_new, :]
        newk_ref[0:WINDOW - t_new, g * HEAD_C:(g + 1) * HEAD_C] = k_old[t_new:, :]
        newk_ref[WINDOW - t_new:, g * HEAD_C:(g + 1) * HEAD_C] = k_new
        newv_ref[0:WINDOW - t_new, g * HEAD_C:(g + 1) * HEAD_C] = v_old[t_new:, :]
        newv_ref[WINDOW - t_new:, g * HEAD_C:(g + 1) * HEAD_C] = v_new


def attn_sample(proj, cache_k, cache_v, sinks, g_qn, g_kn, prev_out, *, batch, seq, start, row0, q_col, k_col,
                q_heads, kv_heads):
    qw = q_heads * HEAD_C
    kvw = kv_heads * HEAD_C
    c_tab, s_tab = _rope_tables(start + jnp.arange(seq))
    kern = functools.partial(_attn_sample_kernel, kv_heads=kv_heads, group=q_heads // kv_heads, t_new=seq)
    blk0 = row0 // seq
    return pl.pallas_call(
        kern,
        grid=(batch,),
        in_specs=[pl.BlockSpec(memory_space=pltpu.SMEM),
                  pl.BlockSpec((seq, qw), lambda b: (blk0 + b, q_col // qw)),
                  pl.BlockSpec((seq, 2 * kvw), lambda b: (blk0 + b, k_col // (2 * kvw))),
                  pl.BlockSpec((None, WINDOW, kvw), lambda b: (b, 0, 0)),
                  pl.BlockSpec((None, WINDOW, kvw), lambda b: (b, 0, 0)),
                  pl.BlockSpec((seq, HEAD_C), lambda b: (0, 0)),
                  pl.BlockSpec((seq, HEAD_C), lambda b: (0, 0)),
                  pl.BlockSpec((1, HEAD_C), lambda b: (0, 0)),
                  pl.BlockSpec((1, HEAD_C), lambda b: (0, 0)),
                  pl.BlockSpec(memory_space=pl.ANY)],
        out_specs=[pl.BlockSpec((seq, qw), lambda b: (blk0 + b, 0)),
                   pl.BlockSpec((None, WINDOW, kvw), lambda b: (b, 0, 0)),
                   pl.BlockSpec((None, WINDOW, kvw), lambda b: (b, 0, 0))],
        out_shape=[jax.ShapeDtypeStruct(prev_out.shape, F32),
                   jax.ShapeDtypeStruct((batch, WINDOW, kvw), F32),
                   jax.ShapeDtypeStruct((batch, WINDOW, kvw), F32)],
        input_output_aliases={9: 0},
        compiler_params=_params("parallel"),
        name="attn_sample",
    )(sinks, proj, proj, cache_k, cache_v, c_tab, s_tab, g_qn.reshape(1, HEAD_C), g_kn.reshape(1, HEAD_C),
      prev_out)


def _ffn_up_kernel(te_ref, ns_ref, ts_ref, x_ref, w1_ref, w3_ref, o_ref, w1b_ref, w3b_ref, *, sub, n_sub):
    n_valid = ns_ref[pl.program_id(0)]

    @pl.when(n_valid == n_sub)
    def _():
        x = x_ref[...]
        a = jnp.dot(x, w1_ref[...].astype(BF16), preferred_element_type=F32)
        b = jnp.dot(x, w3_ref[...].astype(BF16), preferred_element_type=F32)
        o_ref[...] = (_silu(a) * b).astype(o_ref.dtype)

    @pl.when((n_valid > 0) & (n_valid < n_sub))
    def _():
        w1b_ref[...] = w1_ref[...].astype(BF16)
        w3b_ref[...] = w3_ref[...].astype(BF16)

        def compute(s, carry):
            rows = pl.ds(pl.multiple_of(s * sub, sub), sub)
            x = x_ref[rows, :]
            a = jnp.dot(x, w1b_ref[...], preferred_element_type=F32)
            b = jnp.dot(x, w3b_ref[...], preferred_element_type=F32)
            o_ref[rows, :] = (_silu(a) * b).astype(o_ref.dtype)
            return carry

        def clear(s, carry):
            rows = pl.ds(pl.multiple_of(s * sub, sub), sub)
            o_ref[rows, :] = jnp.zeros((sub, o_ref.shape[1]), o_ref.dtype)
            return carry

        lax.fori_loop(0, n_valid, compute, 0)
        lax.fori_loop(n_valid, n_sub, clear, 0)

    @pl.when(n_valid == 0)
    def _():
        o_ref[...] = jnp.zeros_like(o_ref)


def ffn_up(x, w1, w3, tile_expert, tile_nsub, tile_src, *, tile, sub):
    m, k = x.shape
    f = w1.shape[-1]
    tf = _pick(f, (256, 128))
    nf = f // tf

    def col(i, j, ns):
        return jnp.where(ns[i] > 0, j, nf - 1)

    return pl.pallas_call(
        functools.partial(_ffn_up_kernel, sub=sub, n_sub=tile // sub),
        grid_spec=pltpu.PrefetchScalarGridSpec(
            num_scalar_prefetch=3,
            grid=(m // tile, nf),
            in_specs=[pl.BlockSpec((tile, k), lambda i, j, te, ns, ts: (ts[i], 0), pipeline_mode=pl.Buffered(1)),
                      pl.BlockSpec((None, k, tf), lambda i, j, te, ns, ts: (te[i], 0, col(i, j, ns))),
                      pl.BlockSpec((None, k, tf), lambda i, j, te, ns, ts: (te[i], 0, col(i, j, ns)))],
            out_specs=pl.BlockSpec((tile, tf), lambda i, j, te, ns, ts: (i, j)),
            scratch_shapes=[pltpu.VMEM((k, tf), BF16), pltpu.VMEM((k, tf), BF16)]),
        out_shape=jax.ShapeDtypeStruct((m, f), BF16),
        compiler_params=_params("arbitrary", "arbitrary"),
        name="ffn_up",
    )(tile_expert, tile_nsub, tile_src, x, w1, w3)


def _ffn_down_kernel(te_ref, ns_ref, ts_ref, g_ref, w2_ref, *rest, sub, n_sub, with_residual):
    if with_residual:
        r_ref, o_ref, wb_ref = rest
    else:
        o_ref, wb_ref = rest
    n_valid = ns_ref[pl.program_id(0)]

    @pl.when((n_valid > 0) & (pl.program_id(2) == 0))
    def _():
        o_ref[...] = r_ref[...] if with_residual else jnp.zeros_like(o_ref)

    @pl.when(n_valid == n_sub)
    def _():
        o_ref[...] += jnp.dot(g_ref[...], w2_ref[...].astype(BF16), preferred_element_type=F32)

    @pl.when((n_valid > 0) & (n_valid < n_sub))
    def _():
        wb_ref[...] = w2_ref[...].astype(BF16)

        def compute(s, carry):
            rows = pl.ds(pl.multiple_of(s * sub, sub), sub)
            o_ref[rows, :] += jnp.dot(g_ref[rows, :], wb_ref[...], preferred_element_type=F32)
            return carry

        lax.fori_loop(0, n_valid, compute, 0)

    @pl.when((n_valid == 0) & (pl.program_id(2) == 0))
    def _():
        o_ref[...] = jnp.zeros_like(o_ref)


def ffn_down(g, w2, tile_expert, tile_nsub, tile_src, *, tile, sub, residual=None):
    m, f = g.shape
    n = w2.shape[-1]
    tn = _pick(n, (1024, 512, 256, 128))
    tk = _pick(f, (1024, 896, 512, 256, 128))
    n_j, n_k = n // tn, f // tk

    def col(i, j, ns):
        return jnp.where(ns[i] > 0, j, n_j - 1)

    def red(i, kk, ns):
        return jnp.where(ns[i] > 0, kk, n_k - 1)

    in_specs = [pl.BlockSpec((tile, tk), lambda i, j, kk, te, ns, ts: (ts[i], red(i, kk, ns))),
                pl.BlockSpec((None, tk, tn), lambda i, j, kk, te, ns, ts: (te[i], red(i, kk, ns), col(i, j, ns)))]
    args = [tile_expert, tile_nsub, tile_src, g, w2]
    if residual is not None:
        in_specs.append(pl.BlockSpec((tile, tn), lambda i, j, kk, te, ns, ts: (ts[i], col(i, j, ns)),
                                     pipeline_mode=pl.Buffered(1)))
        args.append(residual)
    return pl.pallas_call(
        functools.partial(_ffn_down_kernel, sub=sub, n_sub=tile // sub, with_residual=residual is not None),
        grid_spec=pltpu.PrefetchScalarGridSpec(
            num_scalar_prefetch=3,
            grid=(m // tile, n_j, n_k),
            in_specs=in_specs,
            out_specs=pl.BlockSpec((tile, tn), lambda i, j, kk, te, ns, ts: (i, j)),
            scratch_shapes=[pltpu.VMEM((tk, tn), BF16)]),
        out_shape=jax.ShapeDtypeStruct((m, n), F32),
        compiler_params=_params("arbitrary", "arbitrary", "arbitrary"),
        name="ffn_down",
    )(*args)


def _router_kernel(x_ref, g_ref, w_ref, h_ref, idx_ref, gate_ref, *, n_experts):
    x = x_ref[...]
    h = x * lax.rsqrt(jnp.mean(x * x, axis=-1, keepdims=True) + RMS_EPS) * g_ref[...]
    h_ref[...] = h
    logits = jnp.dot(h, w_ref[...], precision=HIGHEST, preferred_element_type=F32)
    lane = lax.broadcasted_iota(jnp.int32, logits.shape, 1)
    logits = jnp.where(lane < n_experts, logits, -jnp.inf)
    v1 = jnp.max(logits, axis=-1, keepdims=True)
    i1 = jnp.min(jnp.where(logits == v1, lane, LANES), axis=-1, keepdims=True)
    rest = jnp.where(lane == i1, -jnp.inf, logits)
    v2 = jnp.max(rest, axis=-1, keepdims=True)
    i2 = jnp.min(jnp.where(rest == v2, lane, LANES), axis=-1, keepdims=True)
    e2 = jnp.exp(v2 - v1)
    g1 = 1.0 / (1.0 + e2)
    g2 = e2 / (1.0 + e2)
    idx_ref[...] = jnp.where(lane == 0, i1, jnp.where(lane == 1, i2, 0))
    gate_ref[...] = jnp.where(lane == 0, g1, jnp.where(lane == 1, g2, 0.0))


def router(x, g_ffn, w_router):
    m, d = x.shape
    n_experts = w_router.shape[-1]
    tm = _pick(m, (256, 128, 64, 32, 16))
    w_pad = jnp.zeros((d, LANES), F32).at[:, :n_experts].set(w_router)
    kern = functools.partial(_router_kernel, n_experts=n_experts)
    h, idx, gate = pl.pallas_call(
        kern,
        grid=(m // tm,),
        in_specs=[pl.BlockSpec((tm, d), lambda i: (i, 0)),
                  pl.BlockSpec((1, d), lambda i: (0, 0)),
                  pl.BlockSpec((d, LANES), lambda i: (0, 0))],
        out_specs=[pl.BlockSpec((tm, d), lambda i: (i, 0)),
                   pl.BlockSpec((tm, LANES), lambda i: (i, 0)),
                   pl.BlockSpec((tm, LANES), lambda i: (i, 0))],
        out_shape=[jax.ShapeDtypeStruct((m, d), F32),
                   jax.ShapeDtypeStruct((m, LANES), jnp.int32),
                   jax.ShapeDtypeStruct((m, LANES), F32)],
        compiler_params=_params("parallel"),
        name="router",
    )(x, g_ffn.reshape(1, d), w_pad)
    return h, idx[:, :TOP_K], gate[:, :TOP_K]


def _row_copy(src_ref, dst_ref, sem, src_row, dst_row):
    return pltpu.make_async_copy(src_ref.at[pl.ds(src_row, 1)], dst_ref.at[pl.ds(dst_row, 1)], sem)


def _dispatch_kernel(idx_ref, valid_ref, src_ref, o_ref, buf_ref, sem, *, rows):
    i = pl.program_id(0)

    @pl.when(valid_ref[i] > 0)
    def _():
        base = i * rows

        def start(r, carry):
            _row_copy(src_ref, buf_ref, sem, idx_ref[base + r], r).start()
            return carry

        def wait(r, carry):
            _row_copy(src_ref, buf_ref, sem, idx_ref[base + r], r).wait()
            return carry

        lax.fori_loop(0, rows, start, 0)
        lax.fori_loop(0, rows, wait, 0)
        o_ref[...] = buf_ref[...].astype(o_ref.dtype)

    @pl.when(valid_ref[i] == 0)
    def _():
        o_ref[...] = jnp.zeros_like(o_ref)


def dispatch_rows(src, idx, block_valid, rows):
    n = idx.shape[0]
    d = src.shape[1]
    return pl.pallas_call(
        functools.partial(_dispatch_kernel, rows=rows),
        grid_spec=pltpu.PrefetchScalarGridSpec(
            num_scalar_prefetch=2,
            grid=(n // rows,),
            in_specs=[pl.BlockSpec(memory_space=pl.ANY)],
            out_specs=pl.BlockSpec((rows, d), lambda i, idx, valid: (i, 0)),
            scratch_shapes=[pltpu.VMEM((rows, d), src.dtype), pltpu.SemaphoreType.DMA(())]),
        out_shape=jax.ShapeDtypeStruct((n, d), BF16),
        compiler_params=_params("arbitrary"),
        name="dispatch_rows",
    )(idx, block_valid, src)


def _combine_kernel(slot_ref, x_ref, gate_ref, ys_ref, o_ref, ya_ref, yb_ref, sems, *, rows):
    base = pl.program_id(0) * rows

    def copies(r):
        token = base + r
        return (_row_copy(ys_ref, ya_ref, sems.at[0], slot_ref[TOP_K * token], r),
                _row_copy(ys_ref, yb_ref, sems.at[1], slot_ref[TOP_K * token + 1], r))

    def start(r, carry):
        for cp in copies(r):
            cp.start()
        return carry

    def wait(r, carry):
        for cp in copies(r):
            cp.wait()
        return carry

    lax.fori_loop(0, rows, start, 0)
    lax.fori_loop(0, rows, wait, 0)
    gate = gate_ref[...]
    o_ref[...] = x_ref[...] + (gate[:, 0:1] * ya_ref[...] + gate[:, 1:2] * yb_ref[...])


def moe_combine(x, ys, slot, gate):
    m, d = x.shape
    rows = _pick(m, (256, 128, 64, 32, 16, 8))
    gate_pad = jnp.zeros((m, LANES), F32).at[:, :TOP_K].set(gate)
    return pl.pallas_call(
        functools.partial(_combine_kernel, rows=rows),
        grid_spec=pltpu.PrefetchScalarGridSpec(
            num_scalar_prefetch=1,
            grid=(m // rows,),
            in_specs=[pl.BlockSpec((rows, d), lambda i, slot: (i, 0)),
                      pl.BlockSpec((rows, LANES), lambda i, slot: (i, 0)),
                      pl.BlockSpec(memory_space=pl.ANY)],
            out_specs=pl.BlockSpec((rows, d), lambda i, slot: (i, 0)),
            scratch_shapes=[pltpu.VMEM((rows, d), F32), pltpu.VMEM((rows, d), F32),
                            pltpu.SemaphoreType.DMA((TOP_K,))]),
        out_shape=jax.ShapeDtypeStruct((m, d), F32),
        compiler_params=_params("arbitrary"),
        name="moe_combine",
    )(slot, x, gate_pad, ys)


MOE_SUB_ROWS = 256
MOE_TILE_HEADROOM = 1.05


def moe_ffn(x, g_ffn, w_router, w1, w3, w2, which):
    m, d = x.shape
    n_experts = w_router.shape[-1]
    w1, w3, w2 = (w.reshape((-1,) + w.shape[2:]) for w in (w1, w3, w2))
    h, top_i, top_w = router(x, g_ffn, w_router)

    n_assign = m * TOP_K
    sub = MOE_SUB_ROWS if n_assign >= 32 * MOE_SUB_ROWS else 32
    n_sub = -(-int(MOE_TILE_HEADROOM * n_assign / n_experts) // sub)
    tile = n_sub * sub
    n_tiles = (n_assign + n_experts * (tile - 1)) // tile
    flat_e = top_i.reshape(-1)
    counts = jnp.sum(jax.nn.one_hot(flat_e, n_experts, dtype=jnp.int32), axis=0)
    subs_per = (counts + sub - 1) // sub
    tiles_per = (subs_per + n_sub - 1) // n_sub
    tile_end = jnp.cumsum(tiles_per)
    tile_begin = tile_end - tiles_per
    order = jnp.argsort(flat_e, stable=True)
    sorted_e = flat_e[order]
    rank = jnp.arange(n_assign, dtype=jnp.int32) - (jnp.cumsum(counts) - counts)[sorted_e]
    slot_sorted = tile_begin[sorted_e] * tile + rank
    slot = jnp.zeros((n_assign,), jnp.int32).at[order].set(slot_sorted)
    row_token = jnp.zeros((n_tiles * tile,), jnp.int32).at[slot_sorted].set((order // TOP_K).astype(jnp.int32))
    tile_ids = jnp.arange(n_tiles, dtype=jnp.int32)
    n_used = tile_end[-1]
    owner = jnp.minimum(jnp.searchsorted(tile_end, tile_ids, side="right"), n_experts - 1).astype(jnp.int32)
    tile_nsub = jnp.clip(subs_per[owner] - (tile_ids - tile_begin[owner]) * n_sub, 0, n_sub)
    tile_nsub = jnp.where(tile_ids < n_used, tile_nsub, 0).astype(jnp.int32)
    tile_src = jnp.minimum(tile_ids, n_used - 1).astype(jnp.int32)
    tile_expert = owner[tile_src] + which * n_experts
    sub_valid = (jnp.arange(n_sub, dtype=jnp.int32)[None, :] < tile_nsub[:, None]).astype(jnp.int32).reshape(-1)

    xs = dispatch_rows(h, row_token, sub_valid, sub)
    g = ffn_up(xs, w1, w3, tile_expert, tile_nsub, tile_src, tile=tile, sub=sub)
    ys = ffn_down(g, w2, tile_expert, tile_nsub, tile_src, tile=tile, sub=sub)
    return moe_combine(x, ys, slot, top_w)


def dense_ffn(x, g_ffn, w1, w3, w2, which):
    m = x.shape[0]
    h = rmsnorm_rows(x, g_ffn, BF16)
    tile = _pick(m, (2112, 2048, 1024, 528, 512, 256, 128, 64, 32, 16))
    sub = tile
    n_tiles = m // tile
    tile_expert = jnp.full((n_tiles,), which, jnp.int32)
    tile_nsub = jnp.full((n_tiles,), tile // sub, jnp.int32)
    tile_src = jnp.arange(n_tiles, dtype=jnp.int32)
    g = ffn_up(h, w1, w3, tile_expert, tile_nsub, tile_src, tile=tile, sub=sub)
    return ffn_down(g, w2, tile_expert, tile_nsub, tile_src, tile=tile, sub=sub, residual=x)


def kernel(x_prompt, x_sample, state_hgrn, state_conv, cache_k, cache_v, lb_param, g_mix, w_in, g_hgrn, w_out_a, w_dw, b_dw, ln_g, ln_b, w_out_b, g_qn, g_kn, sinks, w_out_c, w_o, g_ffn, w1_dense, w3_dense, w2_dense, w_router, w1_moe, w3_moe, w2_moe):
    bp, tp, d = x_prompt.shape
    bs, ts, _ = x_sample.shape
    depth = w_in.shape[0]
    heads_a = state_hgrn.shape[2]
    channels_b = state_conv.shape[-1]
    conv_w = w_dw.shape[1]
    kv_heads = cache_k.shape[3]
    q_heads = sinks.shape[1]
    n_prompt = bp * tp
    m_rows = n_prompt + bs * ts
    past_len = PAST_LEN

    a_cols = heads_a * HEAD_A
    lin_col = 4 * a_cols
    gate_b_col = lin_col + channels_b
    q_col = gate_b_col + channels_b
    k_col = q_col + q_heads * HEAD_C
    merge_gate_col = k_col + 2 * kv_heads * HEAD_C

    p_soft = jax.nn.softmax(lb_param.astype(F32), axis=0)
    lower = jnp.cumsum(p_soft, axis=0) - p_soft[0:1]

    x = jnp.concatenate([x_prompt.reshape(n_prompt, d), x_sample.reshape(bs * ts, d)], axis=0)
    zero_state = jnp.zeros((bp, heads_a, HEAD_A, HEAD_A), F32)
    zero_hist = jnp.zeros((bp, HIST_ROWS, channels_b), F32)
    outs = {name: [] for name in ("hp", "cp", "kp", "vp", "hs", "cs", "ks", "vs")}

    for l in range(depth):
        h = rmsnorm_rows(x, g_mix[l], BF16)
        proj = gemm(h, w_in, l)

        o_a, s_p = hgrn_mixer(proj, lower[l], g_hgrn[l], zero_state, batch=bp, seq=tp,
                              chunk=_pick(tp, (128, 64, 32, 16, 8)), row0=0, heads=heads_a, total_rows=m_rows,
                              prev_out=jnp.zeros((m_rows, a_cols), F32))
        o_a, s_s = hgrn_mixer(proj, lower[l], g_hgrn[l], state_hgrn[l], batch=bs, seq=ts,
                              chunk=_pick(ts, (128, 64, 32, 16, 8)), row0=n_prompt, heads=heads_a, total_rows=m_rows,
                              prev_out=o_a)

        hist_s = jnp.pad(state_conv[l], ((0, 0), (HIST_ROWS - (conv_w - 1), 0), (0, 0)))
        conv, c_p = conv_mixer(proj, zero_hist, w_dw[l], b_dw[l], batch=bp, seq=tp,
                               tb=_pick(tp, (128, 64, 32, 16, 8)), row0=0, lin_col=lin_col, gate_col=gate_b_col,
                               total_rows=m_rows, prev_out=jnp.zeros((m_rows, channels_b), F32))
        conv, c_s = conv_mixer(proj, hist_s, w_dw[l], b_dw[l], batch=bs, seq=ts,
                               tb=_pick(ts, (128, 64, 32, 16, 8)), row0=n_prompt, lin_col=lin_col,
                               gate_col=gate_b_col, total_rows=m_rows, prev_out=conv)
        o_b = layernorm_silu_rows(conv, ln_g[l], ln_b[l], BF16)

        o_c, k_p, v_p = attn_prompt(proj, sinks[l], g_qn[l], g_kn[l], jnp.zeros((m_rows, q_heads * HEAD_C), F32),
                                    batch=bp, seq=tp, q_col=q_col, k_col=k_col, q_heads=q_heads, kv_heads=kv_heads)
        o_c, k_s, v_s = attn_sample(proj, cache_k[l].reshape(bs, WINDOW, kv_heads * HEAD_C),
                                    cache_v[l].reshape(bs, WINDOW, kv_heads * HEAD_C), sinks[l], g_qn[l], g_kn[l],
                                    o_c, batch=bs, seq=ts, start=past_len, row0=n_prompt, q_col=q_col, k_col=k_col,
                                    q_heads=q_heads, kv_heads=kv_heads)

        mixed = gated_merge(o_a, o_b, o_c, w_out_a, w_out_b, w_out_c, proj, l, merge_gate_col, d)
        x = gemm(mixed, w_o, l, residual=x)

        j = l // 2
        if l % 2 == 0:
            x = dense_ffn(x, g_ffn[l], w1_dense, w3_dense, w2_dense, j)
        else:
            x = moe_ffn(x, g_ffn[l], w_router[j], w1_moe, w3_moe, w2_moe, j)

        outs["hp"].append(s_p)
        outs["cp"].append(c_p)
        outs["kp"].append(k_p.reshape(bp, WINDOW, kv_heads, HEAD_C))
        outs["vp"].append(v_p.reshape(bp, WINDOW, kv_heads, HEAD_C))
        outs["hs"].append(s_s)
        outs["cs"].append(c_s)
        outs["ks"].append(k_s.reshape(bs, WINDOW, kv_heads, HEAD_C))
        outs["vs"].append(v_s.reshape(bs, WINDOW, kv_heads, HEAD_C))

    y_prompt = x[:n_prompt].reshape(bp, tp, d)
    y_sample = x[n_prompt:].reshape(bs, ts, d)
    return (y_prompt, y_sample,
            jnp.stack(outs["hp"]), jnp.stack(outs["cp"]), jnp.stack(outs["kp"]), jnp.stack(outs["vp"]),
            jnp.stack(outs["hs"]), jnp.stack(outs["cs"]), jnp.stack(outs["ks"]), jnp.stack(outs["vs"]))
```

```python
import functools

import jax
import jax.numpy as jnp
from jax import lax
from jax.experimental import pallas as pl
from jax.experimental.pallas import tpu as pltpu

F32 = jnp.float32
BF16 = jnp.bfloat16

RMS_EPS = 1e-6
LN_EPS = 1e-5
F_MIN = 1e-30
NEG_BIG = -1e30
ROPE_THETA = 500000.0
HEAD_A = 128
HEAD_C = 64
WINDOW = 128
PAST_LEN = 16384
TOP_K = 2
HIGHEST = lax.Precision.HIGHEST

V7X_VMEM_LIMIT_BYTES = 56 * 1024 * 1024
LANES = 128
SUBLANES = 8


def _pick(n, candidates):
    for c in candidates:
        if n % c == 0:
            return c
    raise ValueError(f"no tile in {candidates} divides {n}")


def _params(*semantics):
    return pltpu.CompilerParams(dimension_semantics=semantics, vmem_limit_bytes=V7X_VMEM_LIMIT_BYTES)


def _silu(x):
    return x * jax.nn.sigmoid(x)


def _rmsnorm_kernel(x_ref, g_ref, o_ref):
    x = x_ref[...]
    ms = jnp.mean(x * x, axis=-1, keepdims=True)
    o_ref[...] = (x * lax.rsqrt(ms + RMS_EPS) * g_ref[...]).astype(o_ref.dtype)


def rmsnorm_rows(x, g, out_dtype):
    m, d = x.shape
    tm = _pick(m, (256, 128, 64, 32, 16))
    return pl.pallas_call(
        _rmsnorm_kernel,
        grid=(m // tm,),
        in_specs=[pl.BlockSpec((tm, d), lambda i: (i, 0)), pl.BlockSpec((1, d), lambda i: (0, 0))],
        out_specs=pl.BlockSpec((tm, d), lambda i: (i, 0)),
        out_shape=jax.ShapeDtypeStruct((m, d), out_dtype),
        compiler_params=_params("parallel"),
        name="rmsnorm_rows",
    )(x, g.reshape(1, d))


def _ln_silu_kernel(x_ref, g_ref, b_ref, o_ref):
    x = x_ref[...]
    mu = jnp.mean(x, axis=-1, keepdims=True)
    xc = x - mu
    var = jnp.mean(xc * xc, axis=-1, keepdims=True)
    y = xc * lax.rsqrt(var + LN_EPS) * g_ref[...] + b_ref[...]
    o_ref[...] = _silu(y).astype(o_ref.dtype)


def layernorm_silu_rows(x, g, b, out_dtype):
    m, d = x.shape
    tm = _pick(m, (256, 128, 64, 32, 16))
    return pl.pallas_call(
        _ln_silu_kernel,
        grid=(m // tm,),
        in_specs=[pl.BlockSpec((tm, d), lambda i: (i, 0)),
                  pl.BlockSpec((1, d), lambda i: (0, 0)),
                  pl.BlockSpec((1, d), lambda i: (0, 0))],
        out_specs=pl.BlockSpec((tm, d), lambda i: (i, 0)),
        out_shape=jax.ShapeDtypeStruct((m, d), out_dtype),
        compiler_params=_params("parallel"),
        name="layernorm_silu_rows",
    )(x, g.reshape(1, d), b.reshape(1, d))


def _gemm_kernel(x_ref, w_ref, o_ref):
    o_ref[...] = jnp.dot(x_ref[...].astype(BF16), w_ref[...].astype(BF16), preferred_element_type=F32)


def _gemm_residual_kernel(x_ref, w_ref, r_ref, o_ref):
    o_ref[...] = r_ref[...] + jnp.dot(x_ref[...].astype(BF16), w_ref[...].astype(BF16),
                                      preferred_element_type=F32)


def gemm(x, w, layer, residual=None):
    m, k = x.shape
    n = w.shape[-1]
    tm = _pick(m, (2112, 2048, 1024, 512, 256, 128, 64, 32, 16))
    tn = _pick(n, (512, 256, 128) if residual is None else (256, 128))
    in_specs = [pl.BlockSpec((tm, k), lambda i, j: (i, 0), pipeline_mode=pl.Buffered(1)),
                pl.BlockSpec((None, k, tn), lambda i, j: (layer, 0, j))]
    args = [x, w]
    body = _gemm_kernel
    if residual is not None:
        in_specs.append(pl.BlockSpec((tm, tn), lambda i, j: (i, j)))
        args.append(residual)
        body = _gemm_residual_kernel
    return pl.pallas_call(
        body,
        grid=(m // tm, n // tn),
        in_specs=in_specs,
        out_specs=pl.BlockSpec((tm, tn), lambda i, j: (i, j)),
        out_shape=jax.ShapeDtypeStruct((m, n), F32),
        compiler_params=_params("parallel", "arbitrary"),
        name="gemm_residual" if residual is not None else "gemm",
    )(*args)


def _merge_kernel(oa_ref, ob_ref, oc_ref, wa_ref, wb_ref, wc_ref, ga_ref, gb_ref, gc_ref, o_ref):
    def branch(o_ref_, w_ref_, g_ref_):
        y = jnp.dot(o_ref_[...].astype(BF16), w_ref_[...].astype(BF16), preferred_element_type=F32)
        return jax.nn.sigmoid(g_ref_[...]) * y

    mixed = branch(oa_ref, wa_ref, ga_ref) + branch(ob_ref, wb_ref, gb_ref) + branch(oc_ref, wc_ref, gc_ref)
    o_ref[...] = mixed.astype(o_ref.dtype)


def gated_merge(o_a, o_b, o_c, w_out_a, w_out_b, w_out_c, proj, layer, gate_col, d_model):
    m = o_a.shape[0]
    tm = _pick(m, (1056, 1024, 512, 256, 128, 64, 32, 16))
    tn = next(c for c in (256, 128) if d_model % c == 0 and gate_col % c == 0)
    gate_blk = gate_col // tn
    per_gate = d_model // tn

    def o_spec(o):
        return pl.BlockSpec((tm, o.shape[1]), lambda i, j: (i, 0), pipeline_mode=pl.Buffered(1))

    def w_spec(w):
        return pl.BlockSpec((None, w.shape[1], tn), lambda i, j: (layer, 0, j))

    def g_spec(which):
        return pl.BlockSpec((tm, tn), lambda i, j: (i, gate_blk + which * per_gate + j))

    return pl.pallas_call(
        _merge_kernel,
        grid=(m // tm, d_model // tn),
        in_specs=[o_spec(o_a), o_spec(o_b), o_spec(o_c), w_spec(w_out_a), w_spec(w_out_b), w_spec(w_out_c),
                  g_spec(0), g_spec(1), g_spec(2)],
        out_specs=pl.BlockSpec((tm, tn), lambda i, j: (i, j)),
        out_shape=jax.ShapeDtypeStruct((m, d_model), BF16),
        compiler_params=_params("parallel", "arbitrary"),
        name="gated_merge",
    )(o_a, o_b, o_c, w_out_a, w_out_b, w_out_c, proj, proj, proj)


def _cumsum_rows(x):
    n = x.shape[0]
    row = lax.broadcasted_iota(jnp.int32, x.shape, 0)
    shift = 1
    while shift < n:
        x = x + jnp.where(row >= shift, pltpu.roll(x, shift, axis=0), 0.0)
        shift *= 2
    return x


def _bf16_dot_t(a, b):
    return lax.dot_general(a.astype(BF16), b.astype(BF16), (((1,), (1,)), ((), ())), preferred_element_type=F32)


def _hgrn_head(q_in, f_in, v, gate_in, lb, gain, st, *, chunk):
    sig = jax.nn.sigmoid(f_in)
    f_gate = lb + (1.0 - lb) * sig
    log_f = jnp.log(jnp.maximum(f_gate, F_MIN))
    k = (1.0 - lb) * (1.0 - sig)
    q = _silu(q_in)
    b = _cumsum_rows(log_f)

    o = _bf16_dot_t(q * jnp.exp(b), st)

    n_sub = chunk // SUBLANES
    q3, k3, b3, v3 = (a.reshape(n_sub, SUBLANES, HEAD_A) for a in (q, k, b, v))
    r8 = lax.broadcasted_iota(jnp.int32, (n_sub, SUBLANES, 1), 1)
    acc = jnp.zeros((n_sub, SUBLANES, HEAD_A), F32)
    for s8 in range(SUBLANES):
        causal = r8 >= s8
        decay = jnp.exp(jnp.where(causal, b3 - b3[:, s8:s8 + 1, :], 0.0))
        score = jnp.sum(jnp.where(causal, q3 * k3[:, s8:s8 + 1, :] * decay, 0.0), axis=-1, keepdims=True)
        acc = acc + score * v3[:, s8:s8 + 1, :]
    o = o + acc.reshape(chunk, HEAD_A)

    row = lax.broadcasted_iota(jnp.int32, (chunk, 1), 0)
    scores = None
    m = SUBLANES
    while 2 * m <= chunk:
        n_blk = chunk // (2 * m)
        ref_row = b.reshape(n_blk, 2 * m, HEAD_A)[:, m - 1:m, :]
        beta = jnp.broadcast_to(ref_row, (n_blk, 2 * m, HEAD_A)).reshape(chunk, HEAD_A)
        second = lax.rem(row, 2 * m) >= m
        q_m = jnp.where(second, q * jnp.exp(jnp.where(second, b - beta, 0.0)), 0.0)
        k_m = jnp.where(second, 0.0, k * jnp.exp(jnp.where(second, 0.0, beta - b)))
        a = _bf16_dot_t(q_m, k_m)
        if n_blk > 1:
            blk_r = lax.broadcasted_iota(jnp.int32, (chunk, chunk), 0) // (2 * m)
            blk_c = lax.broadcasted_iota(jnp.int32, (chunk, chunk), 1) // (2 * m)
            a = jnp.where(blk_r == blk_c, a, 0.0)
        scores = a if scores is None else scores + a
        m *= 2
    if scores is not None:
        o = o + jnp.dot(scores.astype(BF16), v.astype(BF16), preferred_element_type=F32)

    o_n = o * lax.rsqrt(jnp.mean(o * o, axis=-1, keepdims=True) + RMS_EPS) * gain
    out = o_n * _silu(gate_in)

    b_last = b[chunk - 1:chunk, :]
    k_dec = k * jnp.exp(b_last - b)
    st_new = jnp.exp(b_last) * st + lax.dot_general(v.astype(BF16), k_dec.astype(BF16), (((0,), (0,)), ((), ())),
                                                    preferred_element_type=F32)
    return out, st_new


def _hgrn_kernel(q_ref, f_ref, i_ref, g_ref, lb_ref, gh_ref, s0_ref, *rest, chunk, n_chunks, heads_per_step,
                 aliased):
    o_ref, sfin_ref, st_ref = rest[1:] if aliased else rest
    c = pl.program_id(2)

    @pl.when(c == 0)
    def _():
        for hh in range(heads_per_step):
            st_ref[hh] = s0_ref[hh].T

    for hh in range(heads_per_step):
        cols = slice(hh * HEAD_A, (hh + 1) * HEAD_A)
        out, st_new = _hgrn_head(q_ref[:, cols], f_ref[:, cols], i_ref[:, cols], g_ref[:, cols], lb_ref[:, cols],
                                 gh_ref[...], st_ref[hh], chunk=chunk)
        o_ref[:, cols] = out
        st_ref[hh] = st_new

        @pl.when(c == n_chunks - 1)
        def _(hh=hh, st_new=st_new):
            sfin_ref[hh] = st_new.T


def hgrn_mixer(proj, lb, g_hgrn, s0, *, batch, seq, chunk, row0, heads, total_rows, prev_out=None):
    n_chunks = seq // chunk
    blk0 = row0 // chunk
    hps = _pick(heads, (8, 4, 2, 1))
    width = hps * HEAD_A

    def p_spec(slab):
        return pl.BlockSpec((chunk, width), lambda b, h, c: (blk0 + b * n_chunks + c, slab * (heads // hps) + h))

    in_specs = [p_spec(0), p_spec(1), p_spec(2), p_spec(3),
                pl.BlockSpec((1, width), lambda b, h, c: (0, h)),
                pl.BlockSpec((1, HEAD_A), lambda b, h, c: (0, 0)),
                pl.BlockSpec((None, hps, HEAD_A, HEAD_A), lambda b, h, c: (b, h, 0, 0))]
    args = [proj, proj, proj, proj, lb.reshape(1, heads * HEAD_A), g_hgrn.reshape(1, HEAD_A), s0]
    aliases = {}
    if prev_out is not None:
        in_specs.append(pl.BlockSpec(memory_space=pl.ANY))
        args.append(prev_out)
        aliases = {len(args) - 1: 0}
    kern = functools.partial(_hgrn_kernel, chunk=chunk, n_chunks=n_chunks, heads_per_step=hps,
                             aliased=prev_out is not None)
    return pl.pallas_call(
        kern,
        grid=(batch, heads // hps, n_chunks),
        in_specs=in_specs,
        out_specs=[pl.BlockSpec((chunk, width), lambda b, h, c: (blk0 + b * n_chunks + c, h)),
                   pl.BlockSpec((None, hps, HEAD_A, HEAD_A), lambda b, h, c: (b, h, 0, 0))],
        out_shape=[jax.ShapeDtypeStruct((total_rows, heads * HEAD_A), F32),
                   jax.ShapeDtypeStruct((batch, heads, HEAD_A, HEAD_A), F32)],
        scratch_shapes=[pltpu.VMEM((hps, HEAD_A, HEAD_A), F32)],
        input_output_aliases=aliases,
        compiler_params=_params("parallel", "parallel", "arbitrary"),
        name="hgrn_mixer",
    )(*args)


HIST_ROWS = 32


def _conv_kernel(ul_ref, ug_ref, hist_ref, w_ref, bias_ref, *rest, tb, conv_w, aliased):
    o_ref, state_ref, buf_ref, shifted_ref = rest[1:] if aliased else rest
    t = pl.program_id(2)

    @pl.when(t == 0)
    def _():
        buf_ref[0:HIST_ROWS, :] = hist_ref[...]

    u = ul_ref[...] * jax.nn.sigmoid(ug_ref[...])
    buf_ref[HIST_ROWS:HIST_ROWS + tb, :] = u
    first = HIST_ROWS - (conv_w - 1)
    span = tb + HIST_ROWS - SUBLANES
    for phase in range(1, SUBLANES):
        shifted_ref[phase, 0:span, :] = buf_ref[phase:phase + span, :]
    acc = jnp.zeros(u.shape, F32)
    for j in range(conv_w):
        phase = (first + j) % SUBLANES
        start = first + j - phase
        rows = buf_ref[start:start + tb, :] if phase == 0 else shifted_ref[phase, start:start + tb, :]
        acc = acc + w_ref[j:j + 1, :] * rows
    o_ref[...] = acc + bias_ref[...]
    state_ref[...] = buf_ref[tb + first:tb + HIST_ROWS, :]
    tail = buf_ref[tb:tb + HIST_ROWS, :]
    buf_ref[0:HIST_ROWS, :] = tail


def conv_mixer(proj, hist, w_dw, b_dw, *, batch, seq, tb, row0, lin_col, gate_col, total_rows, prev_out=None):
    conv_w, channels = w_dw.shape
    cw = _pick(channels, (256, 128))
    n_t = seq // tb
    blk0 = row0 // tb
    in_specs = [pl.BlockSpec((tb, cw), lambda b, c, t: (blk0 + b * n_t + t, lin_col // cw + c)),
                pl.BlockSpec((tb, cw), lambda b, c, t: (blk0 + b * n_t + t, gate_col // cw + c)),
                pl.BlockSpec((None, HIST_ROWS, cw), lambda b, c, t: (b, 0, c)),
                pl.BlockSpec((conv_w, cw), lambda b, c, t: (0, c)),
                pl.BlockSpec((1, cw), lambda b, c, t: (0, c))]
    args = [proj, proj, hist, w_dw, b_dw.reshape(1, channels)]
    aliases = {}
    if prev_out is not None:
        in_specs.append(pl.BlockSpec(memory_space=pl.ANY))
        args.append(prev_out)
        aliases = {len(args) - 1: 0}
    kern = functools.partial(_conv_kernel, tb=tb, conv_w=conv_w, aliased=prev_out is not None)
    return pl.pallas_call(
        kern,
        grid=(batch, channels // cw, n_t),
        in_specs=in_specs,
        out_specs=[pl.BlockSpec((tb, cw), lambda b, c, t: (blk0 + b * n_t + t, c)),
                   pl.BlockSpec((None, conv_w - 1, cw), lambda b, c, t: (b, 0, c))],
        out_shape=[jax.ShapeDtypeStruct((total_rows, channels), F32),
                   jax.ShapeDtypeStruct((batch, conv_w - 1, channels), F32)],
        scratch_shapes=[pltpu.VMEM((HIST_ROWS + tb, cw), F32),
                        pltpu.VMEM((SUBLANES, HIST_ROWS + tb - SUBLANES, cw), F32)],
        input_output_aliases=aliases,
        compiler_params=_params("parallel", "parallel", "arbitrary"),
        name="conv_mixer",
    )(*args)


def _rope_tables(pos):
    rot = HEAD_C // 4
    inv_freq = jnp.float32(ROPE_THETA) ** (-jnp.arange(0, rot, 2, dtype=F32) / rot)
    ang = pos.astype(F32)[:, None] * inv_freq[None, :]
    cos, sin = jnp.cos(ang), jnp.sin(ang)
    rest = HEAD_C - rot
    c_tab = jnp.concatenate([cos, cos, jnp.ones((pos.shape[0], rest), F32)], axis=-1)
    s_tab = jnp.concatenate([-sin, sin, jnp.zeros((pos.shape[0], rest), F32)], axis=-1)
    return c_tab, s_tab


def _norm_rope(x, gain, c_tab, s_tab):
    half = HEAD_C // 8
    y = x * lax.rsqrt(jnp.mean(x * x, axis=-1, keepdims=True) + RMS_EPS) * gain
    swapped = jnp.concatenate([y[:, half:2 * half], y[:, 0:half], y[:, 2 * half:]], axis=-1)
    return y * c_tab + swapped * s_tab


PAIR = 2 * HEAD_C


def _norm_rope_pair(x, gain, c_tab, s_tab, lane):
    half = HEAD_C // 8
    low = lane < HEAD_C
    sq = x * x
    ss_low = jnp.sum(jnp.where(low, sq, 0.0), axis=-1, keepdims=True)
    ss_high = jnp.sum(jnp.where(low, 0.0, sq), axis=-1, keepdims=True)
    ms = jnp.where(low, ss_low, ss_high) * (1.0 / HEAD_C)
    y = x * lax.rsqrt(ms + RMS_EPS) * gain
    in_head = lane & (HEAD_C - 1)
    ahead = pltpu.roll(y, PAIR - half, axis=1)
    behind = pltpu.roll(y, half, axis=1)
    swapped = jnp.where(in_head < half, ahead, jnp.where(in_head < 2 * half, behind, y))
    return y * c_tab + swapped * s_tab


def _softmax_sink(logits, valid, sink):
    logits = jnp.where(valid, logits, NEG_BIG)
    m = jnp.maximum(jnp.max(logits, axis=-1, keepdims=True), sink)
    p = jnp.where(valid, jnp.exp(logits - m), 0.0)
    denom = jnp.sum(p, axis=-1, keepdims=True) + jnp.exp(sink - m)
    return p / denom


def _qk_logits(q, k):
    return lax.dot_general(q.astype(BF16), k.astype(BF16), (((1,), (1,)), ((), ())),
                           preferred_element_type=F32) * (HEAD_C ** -0.5)


def _pv(p, v):
    return jnp.dot(p.astype(BF16), v.astype(BF16), preferred_element_type=F32)


def _attn_prompt_kernel(sinks_ref, q_ref, kv_ref, c_ref, s_ref, gq_ref, gk_ref, prev_ref, o_ref, newk_ref, newv_ref,
                        kprev_ref, vprev_ref, *, kv_heads, group):
    del prev_ref
    j = pl.program_id(1)
    kvw = kv_heads * HEAD_C

    @pl.when(j == 0)
    def _():
        kprev_ref[...] = jnp.zeros_like(kprev_ref)
        vprev_ref[...] = jnp.zeros_like(vprev_ref)

    c_tab = c_ref[...]
    s_tab = s_ref[...]
    gain_q = gq_ref[...]
    gain_k = gk_ref[...]
    lane = lax.broadcasted_iota(jnp.int32, (1, PAIR), 1)
    low = lane < HEAD_C
    slabs_per_kv = group // 2
    rows = slabs_per_kv * WINDOW
    row = lax.broadcasted_iota(jnp.int32, (rows, 2 * WINDOW), 0)
    cidx = lax.broadcasted_iota(jnp.int32, (rows, 2 * WINDOW), 1)
    r = lax.rem(row, WINDOW)
    first_key = WINDOW - jnp.minimum(j, 1) * WINDOW
    valid = (cidx > r) & (cidx <= r + WINDOW) & (cidx >= first_key)
    slab_of_row = lax.broadcasted_iota(jnp.int32, (rows, 1), 0) // WINDOW

    def block_diag(own_half_is_low, x):
        own = jnp.where(low if own_half_is_low else ~low, x, 0.0)
        other = pltpu.roll(own, HEAD_C, axis=1)
        top, bottom = (own, other) if own_half_is_low else (other, own)
        return jnp.concatenate([top, bottom], axis=0)

    for kv_slab in range(kv_heads // 2):
        lanes = slice(kv_slab * PAIR, (kv_slab + 1) * PAIR)
        k_new = _norm_rope_pair(kv_ref[:, lanes], gain_k, c_tab, s_tab, lane)
        v_new = kv_ref[:, kvw + kv_slab * PAIR:kvw + (kv_slab + 1) * PAIR]
        keys2 = jnp.concatenate([kprev_ref[kv_slab], k_new], axis=0)
        vals2 = jnp.concatenate([vprev_ref[kv_slab], v_new], axis=0)
        for half in range(2):
            g = 2 * kv_slab + half
            k_blk = block_diag(half == 0, keys2)
            v_blk = block_diag(half == 0, vals2)
            slabs = [g * slabs_per_kv + i for i in range(slabs_per_kv)]
            q_g = jnp.concatenate([_norm_rope_pair(q_ref[:, s * PAIR:(s + 1) * PAIR], gain_q, c_tab, s_tab, lane)
                                   for s in slabs], axis=0)
            logits = _qk_logits(q_g, k_blk)
            probs = []
            for odd in range(2):
                sink = jnp.zeros((rows, 1), F32)
                for i, s in enumerate(slabs):
                    sink = jnp.where(slab_of_row == i, sinks_ref[2 * s + odd], sink)
                probs.append(_softmax_sink(logits[:, odd * 2 * WINDOW:(odd + 1) * 2 * WINDOW], valid, sink))
            o_g = _pv(jnp.concatenate(probs, axis=1), v_blk)
            for i, s in enumerate(slabs):
                o_ref[:, s * PAIR:(s + 1) * PAIR] = o_g[i * WINDOW:(i + 1) * WINDOW, :]
        kprev_ref[kv_slab] = k_new
        vprev_ref[kv_slab] = v_new
        newk_ref[:, lanes] = k_new
        newv_ref[:, lanes] = v_new


def attn_prompt(proj, sinks, g_qn, g_kn, prev_out, *, batch, seq, q_col, k_col, q_heads, kv_heads):
    nb = seq // WINDOW
    qw = q_heads * HEAD_C
    kvw = kv_heads * HEAD_C
    if kv_heads % 2 or (q_heads // kv_heads) % 2:
        raise ValueError("attn_prompt pairs adjacent heads: kv_heads and the group size must be even")
    pair = lambda a: jnp.tile(a, (1, 2))
    c_tab, s_tab = (pair(t) for t in _rope_tables(jnp.arange(seq)))
    kern = functools.partial(_attn_prompt_kernel, kv_heads=kv_heads, group=q_heads // kv_heads)
    return pl.pallas_call(
        kern,
        grid=(batch, nb),
        in_specs=[pl.BlockSpec(memory_space=pltpu.SMEM),
                  pl.BlockSpec((WINDOW, qw), lambda b, j: (b * nb + j, q_col // qw)),
                  pl.BlockSpec((WINDOW, 2 * kvw), lambda b, j: (b * nb + j, k_col // (2 * kvw))),
                  pl.BlockSpec((WINDOW, PAIR), lambda b, j: (j, 0)),
                  pl.BlockSpec((WINDOW, PAIR), lambda b, j: (j, 0)),
                  pl.BlockSpec((1, PAIR), lambda b, j: (0, 0)),
                  pl.BlockSpec((1, PAIR), lambda b, j: (0, 0)),
                  pl.BlockSpec(memory_space=pl.ANY)],
        out_specs=[pl.BlockSpec((WINDOW, qw), lambda b, j: (b * nb + j, 0)),
                   pl.BlockSpec((None, WINDOW, kvw), lambda b, j: (b, 0, 0)),
                   pl.BlockSpec((None, WINDOW, kvw), lambda b, j: (b, 0, 0))],
        out_shape=[jax.ShapeDtypeStruct(prev_out.shape, F32),
                   jax.ShapeDtypeStruct((batch, WINDOW, kvw), F32),
                   jax.ShapeDtypeStruct((batch, WINDOW, kvw), F32)],
        scratch_shapes=[pltpu.VMEM((kv_heads // 2, WINDOW, PAIR), F32), pltpu.VMEM((kv_heads // 2, WINDOW, PAIR), F32)],
        input_output_aliases={7: 0},
        compiler_params=_params("parallel", "arbitrary"),
        name="attn_prompt",
    )(sinks, proj, proj, c_tab, s_tab, pair(g_qn.reshape(1, HEAD_C)), pair(g_kn.reshape(1, HEAD_C)), prev_out)


def _attn_sample_kernel(sinks_ref, q_ref, kv_ref, ck_ref, cv_ref, c_ref, s_ref, gq_ref, gk_ref, prev_ref,
                        o_ref, newk_ref, newv_ref, *, kv_heads, group, t_new):
    del prev_ref
    kvw = kv_heads * HEAD_C
    c_tab = c_ref[...]
    s_tab = s_ref[...]
    rows = group * t_new
    row = lax.broadcasted_iota(jnp.int32, (rows, 2 * WINDOW), 0)
    cidx = lax.broadcasted_iota(jnp.int32, (rows, 2 * WINDOW), 1)
    r = lax.rem(row, t_new)
    valid = ((cidx < WINDOW) & (cidx > r)) | ((cidx >= WINDOW) & (cidx - WINDOW <= r))
    head_in_group = lax.broadcasted_iota(jnp.int32, (rows, 1), 0) // t_new
    filler = jnp.zeros((WINDOW - t_new, HEAD_C), F32)

    for g in range(kv_heads):
        k_new = _norm_rope(kv_ref[:, g * HEAD_C:(g + 1) * HEAD_C], gk_ref[...], c_tab, s_tab)
        v_new = kv_ref[:, kvw + g * HEAD_C:kvw + (g + 1) * HEAD_C]
        k_old = ck_ref[:, g * HEAD_C:(g + 1) * HEAD_C]
        v_old = cv_ref[:, g * HEAD_C:(g + 1) * HEAD_C]
        keys = jnp.concatenate([k_old, k_new, filler], axis=0)
        vals = jnp.concatenate([v_old, v_new, filler], axis=0)
        heads = [g * group + i for i in range(group)]
        q_g = jnp.concatenate([_norm_rope(q_ref[:, h * HEAD_C:(h + 1) * HEAD_C], gq_ref[...], c_tab, s_tab)
                               for h in heads], axis=0)
        sink = jnp.zeros((rows, 1), F32)
        for i, h in enumerate(heads):
            sink = jnp.where(head_in_group == i, sinks_ref[h], sink)
        o_g = _pv(_softmax_sink(_qk_logits(q_g, keys), valid, sink), vals)
        for i, h in enumerate(heads):
            o_ref[:, h * HEAD_C:(h + 1) * HEAD_C] = o_g[i * t_new:(i + 1) * t_new, :]
        newk_ref[0:WINDOW - t_new, g * HEAD_C:(g + 1) * HEAD_C] = k_old[t_new:, :]
        newk_ref[WINDOW - t_new:, g * HEAD_C:(g + 1) * HEAD_C] = k_new
        newv_ref[0:WINDOW - t_new, g * HEAD_C:(g + 1) * HEAD_C] = v_old[t_new:, :]
        newv_ref[WINDOW - t_new:, g * HEAD_C:(g + 1) * HEAD_C] = v_new


def attn_sample(proj, cache_k, cache_v, sinks, g_qn, g_kn, prev_out, *, batch, seq, start, row0, q_col, k_col,
                q_heads, kv_heads):
    qw = q_heads * HEAD_C
    kvw = kv_heads * HEAD_C
    c_tab, s_tab = _rope_tables(start + jnp.arange(seq))
    kern = functools.partial(_attn_sample_kernel, kv_heads=kv_heads, group=q_heads // kv_heads, t_new=seq)
    blk0 = row0 // seq
    return pl.pallas_call(
        kern,
        grid=(batch,),
        in_specs=[pl.BlockSpec(memory_space=pltpu.SMEM),
                  pl.BlockSpec((seq, qw), lambda b: (blk0 + b, q_col // qw)),
                  pl.BlockSpec((seq, 2 * kvw), lambda b: (blk0 + b, k_col // (2 * kvw))),
                  pl.BlockSpec((None, WINDOW, kvw), lambda b: (b, 0, 0)),
                  pl.BlockSpec((None, WINDOW, kvw), lambda b: (b, 0, 0)),
                  pl.BlockSpec((seq, HEAD_C), lambda b: (0, 0)),
                  pl.BlockSpec((seq, HEAD_C), lambda b: (0, 0)),
                  pl.BlockSpec((1, HEAD_C), lambda b: (0, 0)),
                  pl.BlockSpec((1, HEAD_C), lambda b: (0, 0)),
                  pl.BlockSpec(memory_space=pl.ANY)],
        out_specs=[pl.BlockSpec((seq, qw), lambda b: (blk0 + b, 0)),
                   pl.BlockSpec((None, WINDOW, kvw), lambda b: (b, 0, 0)),
                   pl.BlockSpec((None, WINDOW, kvw), lambda b: (b, 0, 0))],
        out_shape=[jax.ShapeDtypeStruct(prev_out.shape, F32),
                   jax.ShapeDtypeStruct((batch, WINDOW, kvw), F32),
                   jax.ShapeDtypeStruct((batch, WINDOW, kvw), F32)],
        input_output_aliases={9: 0},
        compiler_params=_params("parallel"),
        name="attn_sample",
    )(sinks, proj, proj, cache_k, cache_v, c_tab, s_tab, g_qn.reshape(1, HEAD_C), g_kn.reshape(1, HEAD_C),
      prev_out)


def _ffn_up_kernel(te_ref, ns_ref, ts_ref, x_ref, w1_ref, w3_ref, o_ref, w1b_ref, w3b_ref, *, sub, n_sub):
    n_valid = ns_ref[pl.program_id(0)]

    @pl.when(n_valid == n_sub)
    def _():
        x = x_ref[...]
        a = jnp.dot(x, w1_ref[...].astype(BF16), preferred_element_type=F32)
        b = jnp.dot(x, w3_ref[...].astype(BF16), preferred_element_type=F32)
        o_ref[...] = (_silu(a) * b).astype(o_ref.dtype)

    @pl.when((n_valid > 0) & (n_valid < n_sub))
    def _():
        w1b_ref[...] = w1_ref[...].astype(BF16)
        w3b_ref[...] = w3_ref[...].astype(BF16)

        def compute(s, carry):
            rows = pl.ds(pl.multiple_of(s * sub, sub), sub)
            x = x_ref[rows, :]
            a = jnp.dot(x, w1b_ref[...], preferred_element_type=F32)
            b = jnp.dot(x, w3b_ref[...], preferred_element_type=F32)
            o_ref[rows, :] = (_silu(a) * b).astype(o_ref.dtype)
            return carry

        def clear(s, carry):
            rows = pl.ds(pl.multiple_of(s * sub, sub), sub)
            o_ref[rows, :] = jnp.zeros((sub, o_ref.shape[1]), o_ref.dtype)
            return carry

        lax.fori_loop(0, n_valid, compute, 0)
        lax.fori_loop(n_valid, n_sub, clear, 0)

    @pl.when(n_valid == 0)
    def _():
        o_ref[...] = jnp.zeros_like(o_ref)


def ffn_up(x, w1, w3, tile_expert, tile_nsub, tile_src, *, tile, sub):
    m, k = x.shape
    f = w1.shape[-1]
    tf = _pick(f, (256, 128))
    nf = f // tf

    def col(i, j, ns):
        return jnp.where(ns[i] > 0, j, nf - 1)

    return pl.pallas_call(
        functools.partial(_ffn_up_kernel, sub=sub, n_sub=tile // sub),
        grid_spec=pltpu.PrefetchScalarGridSpec(
            num_scalar_prefetch=3,
            grid=(m // tile, nf),
            in_specs=[pl.BlockSpec((tile, k), lambda i, j, te, ns, ts: (ts[i], 0), pipeline_mode=pl.Buffered(1)),
                      pl.BlockSpec((None, k, tf), lambda i, j, te, ns, ts: (te[i], 0, col(i, j, ns))),
                      pl.BlockSpec((None, k, tf), lambda i, j, te, ns, ts: (te[i], 0, col(i, j, ns)))],
            out_specs=pl.BlockSpec((tile, tf), lambda i, j, te, ns, ts: (i, j)),
            scratch_shapes=[pltpu.VMEM((k, tf), BF16), pltpu.VMEM((k, tf), BF16)]),
        out_shape=jax.ShapeDtypeStruct((m, f), BF16),
        compiler_params=_params("arbitrary", "arbitrary"),
        name="ffn_up",
    )(tile_expert, tile_nsub, tile_src, x, w1, w3)


def _ffn_down_kernel(te_ref, ns_ref, ts_ref, g_ref, w2_ref, *rest, sub, n_sub, with_residual):
    if with_residual:
        r_ref, o_ref, wb_ref = rest
    else:
        o_ref, wb_ref = rest
    n_valid = ns_ref[pl.program_id(0)]

    @pl.when((n_valid > 0) & (pl.program_id(2) == 0))
    def _():
        o_ref[...] = r_ref[...] if with_residual else jnp.zeros_like(o_ref)

    @pl.when(n_valid == n_sub)
    def _():
        o_ref[...] += jnp.dot(g_ref[...], w2_ref[...].astype(BF16), preferred_element_type=F32)

    @pl.when((n_valid > 0) & (n_valid < n_sub))
    def _():
        wb_ref[...] = w2_ref[...].astype(BF16)

        def compute(s, carry):
            rows = pl.ds(pl.multiple_of(s * sub, sub), sub)
            o_ref[rows, :] += jnp.dot(g_ref[rows, :], wb_ref[...], preferred_element_type=F32)
            return carry

        lax.fori_loop(0, n_valid, compute, 0)

    @pl.when((n_valid == 0) & (pl.program_id(2) == 0))
    def _():
        o_ref[...] = jnp.zeros_like(o_ref)


def ffn_down(g, w2, tile_expert, tile_nsub, tile_src, *, tile, sub, residual=None):
    m, f = g.shape
    n = w2.shape[-1]
    tn = _pick(n, (1024, 512, 256, 128))
    tk = _pick(f, (1024, 896, 512, 256, 128))
    n_j, n_k = n // tn, f // tk

    def col(i, j, ns):
        return jnp.where(ns[i] > 0, j, n_j - 1)

    def red(i, kk, ns):
        return jnp.where(ns[i] > 0, kk, n_k - 1)

    in_specs = [pl.BlockSpec((tile, tk), lambda i, j, kk, te, ns, ts: (ts[i], red(i, kk, ns))),
                pl.BlockSpec((None, tk, tn), lambda i, j, kk, te, ns, ts: (te[i], red(i, kk, ns), col(i, j, ns)))]
    args = [tile_expert, tile_nsub, tile_src, g, w2]
    if residual is not None:
        in_specs.append(pl.BlockSpec((tile, tn), lambda i, j, kk, te, ns, ts: (ts[i], col(i, j, ns)),
                                     pipeline_mode=pl.Buffered(1)))
        args.append(residual)
    return pl.pallas_call(
        functools.partial(_ffn_down_kernel, sub=sub, n_sub=tile // sub, with_residual=residual is not None),
        grid_spec=pltpu.PrefetchScalarGridSpec(
            num_scalar_prefetch=3,
            grid=(m // tile, n_j, n_k),
            in_specs=in_specs,
            out_specs=pl.BlockSpec((tile, tn), lambda i, j, kk, te, ns, ts: (i, j)),
            scratch_shapes=[pltpu.VMEM((tk, tn), BF16)]),
        out_shape=jax.ShapeDtypeStruct((m, n), F32),
        compiler_params=_params("arbitrary", "arbitrary", "arbitrary"),
        name="ffn_down",
    )(*args)


def _router_kernel(x_ref, g_ref, w_ref, h_ref, idx_ref, gate_ref, *, n_experts):
    x = x_ref[...]
    h = x * lax.rsqrt(jnp.mean(x * x, axis=-1, keepdims=True) + RMS_EPS) * g_ref[...]
    h_ref[...] = h
    logits = jnp.dot(h, w_ref[...], precision=HIGHEST, preferred_element_type=F32)
    lane = lax.broadcasted_iota(jnp.int32, logits.shape, 1)
    logits = jnp.where(lane < n_experts, logits, -jnp.inf)
    v1 = jnp.max(logits, axis=-1, keepdims=True)
    i1 = jnp.min(jnp.where(logits == v1, lane, LANES), axis=-1, keepdims=True)
    rest = jnp.where(lane == i1, -jnp.inf, logits)
    v2 = jnp.max(rest, axis=-1, keepdims=True)
    i2 = jnp.min(jnp.where(rest == v2, lane, LANES), axis=-1, keepdims=True)
    e2 = jnp.exp(v2 - v1)
    g1 = 1.0 / (1.0 + e2)
    g2 = e2 / (1.0 + e2)
    idx_ref[...] = jnp.where(lane == 0, i1, jnp.where(lane == 1, i2, 0))
    gate_ref[...] = jnp.where(lane == 0, g1, jnp.where(lane == 1, g2, 0.0))


def router(x, g_ffn, w_router):
    m, d = x.shape
    n_experts = w_router.shape[-1]
    tm = _pick(m, (256, 128, 64, 32, 16))
    w_pad = jnp.zeros((d, LANES), F32).at[:, :n_experts].set(w_router)
    kern = functools.partial(_router_kernel, n_experts=n_experts)
    h, idx, gate = pl.pallas_call(
        kern,
        grid=(m // tm,),
        in_specs=[pl.BlockSpec((tm, d), lambda i: (i, 0)),
                  pl.BlockSpec((1, d), lambda i: (0, 0)),
                  pl.BlockSpec((d, LANES), lambda i: (0, 0))],
        out_specs=[pl.BlockSpec((tm, d), lambda i: (i, 0)),
                   pl.BlockSpec((tm, LANES), lambda i: (i, 0)),
                   pl.BlockSpec((tm, LANES), lambda i: (i, 0))],
        out_shape=[jax.ShapeDtypeStruct((m, d), F32),
                   jax.ShapeDtypeStruct((m, LANES), jnp.int32),
                   jax.ShapeDtypeStruct((m, LANES), F32)],
        compiler_params=_params("parallel"),
        name="router",
    )(x, g_ffn.reshape(1, d), w_pad)
    return h, idx[:, :TOP_K], gate[:, :TOP_K]


def _row_copy(src_ref, dst_ref, sem, src_row, dst_row):
    return pltpu.make_async_copy(src_ref.at[pl.ds(src_row, 1)], dst_ref.at[pl.ds(dst_row, 1)], sem)


def _dispatch_kernel(idx_ref, valid_ref, src_ref, o_ref, buf_ref, sem, *, rows):
    i = pl.program_id(0)

    @pl.when(valid_ref[i] > 0)
    def _():
        base = i * rows

        def start(r, carry):
            _row_copy(src_ref, buf_ref, sem, idx_ref[base + r], r).start()
            return carry

        def wait(r, carry):
            _row_copy(src_ref, buf_ref, sem, idx_ref[base + r], r).wait()
            return carry

        lax.fori_loop(0, rows, start, 0)
        lax.fori_loop(0, rows, wait, 0)
        o_ref[...] = buf_ref[...].astype(o_ref.dtype)

    @pl.when(valid_ref[i] == 0)
    def _():
        o_ref[...] = jnp.zeros_like(o_ref)


def dispatch_rows(src, idx, block_valid, rows):
    n = idx.shape[0]
    d = src.shape[1]
    return pl.pallas_call(
        functools.partial(_dispatch_kernel, rows=rows),
        grid_spec=pltpu.PrefetchScalarGridSpec(
            num_scalar_prefetch=2,
            grid=(n // rows,),
            in_specs=[pl.BlockSpec(memory_space=pl.ANY)],
            out_specs=pl.BlockSpec((rows, d), lambda i, idx, valid: (i, 0)),
            scratch_shapes=[pltpu.VMEM((rows, d), src.dtype), pltpu.SemaphoreType.DMA(())]),
        out_shape=jax.ShapeDtypeStruct((n, d), BF16),
        compiler_params=_params("arbitrary"),
        name="dispatch_rows",
    )(idx, block_valid, src)


def _combine_kernel(slot_ref, x_ref, gate_ref, ys_ref, o_ref, ya_ref, yb_ref, sems, *, rows):
    base = pl.program_id(0) * rows

    def copies(r):
        token = base + r
        return (_row_copy(ys_ref, ya_ref, sems.at[0], slot_ref[TOP_K * token], r),
                _row_copy(ys_ref, yb_ref, sems.at[1], slot_ref[TOP_K * token + 1], r))

    def start(r, carry):
        for cp in copies(r):
            cp.start()
        return carry

    def wait(r, carry):
        for cp in copies(r):
            cp.wait()
        return carry

    lax.fori_loop(0, rows, start, 0)
    lax.fori_loop(0, rows, wait, 0)
    gate = gate_ref[...]
    o_ref[...] = x_ref[...] + (gate[:, 0:1] * ya_ref[...] + gate[:, 1:2] * yb_ref[...])


def moe_combine(x, ys, slot, gate):
    m, d = x.shape
    rows = _pick(m, (256, 128, 64, 32, 16, 8))
    gate_pad = jnp.zeros((m, LANES), F32).at[:, :TOP_K].set(gate)
    return pl.pallas_call(
        functools.partial(_combine_kernel, rows=rows),
        grid_spec=pltpu.PrefetchScalarGridSpec(
            num_scalar_prefetch=1,
            grid=(m // rows,),
            in_specs=[pl.BlockSpec((rows, d), lambda i, slot: (i, 0)),
                      pl.BlockSpec((rows, LANES), lambda i, slot: (i, 0)),
                      pl.BlockSpec(memory_space=pl.ANY)],
            out_specs=pl.BlockSpec((rows, d), lambda i, slot: (i, 0)),
            scratch_shapes=[pltpu.VMEM((rows, d), F32), pltpu.VMEM((rows, d), F32),
                            pltpu.SemaphoreType.DMA((TOP_K,))]),
        out_shape=jax.ShapeDtypeStruct((m, d), F32),
        compiler_params=_params("arbitrary"),
        name="moe_combine",
    )(slot, x, gate_pad, ys)


MOE_SUB_ROWS = 256
MOE_TILE_HEADROOM = 1.05


def moe_ffn(x, g_ffn, w_router, w1, w3, w2, which):
    m, d = x.shape
    n_experts = w_router.shape[-1]
    w1, w3, w2 = (w.reshape((-1,) + w.shape[2:]) for w in (w1, w3, w2))
    h, top_i, top_w = router(x, g_ffn, w_router)

    n_assign = m * TOP_K
    sub = MOE_SUB_ROWS if n_assign >= 32 * MOE_SUB_ROWS else 32
    n_sub = -(-int(MOE_TILE_HEADROOM * n_assign / n_experts) // sub)
    tile = n_sub * sub
    n_tiles = (n_assign + n_experts * (tile - 1)) // tile
    flat_e = top_i.reshape(-1)
    counts = jnp.sum(jax.nn.one_hot(flat_e, n_experts, dtype=jnp.int32), axis=0)
    subs_per = (counts + sub - 1) // sub
    tiles_per = (subs_per + n_sub - 1) // n_sub
    tile_end = jnp.cumsum(tiles_per)
    tile_begin = tile_end - tiles_per
    order = jnp.argsort(flat_e, stable=True)
    sorted_e = flat_e[order]
    rank = jnp.arange(n_assign, dtype=jnp.int32) - (jnp.cumsum(counts) - counts)[sorted_e]
    slot_sorted = tile_begin[sorted_e] * tile + rank
    slot = jnp.zeros((n_assign,), jnp.int32).at[order].set(slot_sorted)
    row_token = jnp.zeros((n_tiles * tile,), jnp.int32).at[slot_sorted].set((order // TOP_K).astype(jnp.int32))
    tile_ids = jnp.arange(n_tiles, dtype=jnp.int32)
    n_used = tile_end[-1]
    owner = jnp.minimum(jnp.searchsorted(tile_end, tile_ids, side="right"), n_experts - 1).astype(jnp.int32)
    tile_nsub = jnp.clip(subs_per[owner] - (tile_ids - tile_begin[owner]) * n_sub, 0, n_sub)
    tile_nsub = jnp.where(tile_ids < n_used, tile_nsub, 0).astype(jnp.int32)
    tile_src = jnp.minimum(tile_ids, n_used - 1).astype(jnp.int32)
    tile_expert = owner[tile_src] + which * n_experts
    sub_valid = (jnp.arange(n_sub, dtype=jnp.int32)[None, :] < tile_nsub[:, None]).astype(jnp.int32).reshape(-1)

    xs = dispatch_rows(h, row_token, sub_valid, sub)
    g = ffn_up(xs, w1, w3, tile_expert, tile_nsub, tile_src, tile=tile, sub=sub)
    ys = ffn_down(g, w2, tile_expert, tile_nsub, tile_src, tile=tile, sub=sub)
    return moe_combine(x, ys, slot, top_w)


def dense_ffn(x, g_ffn, w1, w3, w2, which):
    m = x.shape[0]
    h = rmsnorm_rows(x, g_ffn, BF16)
    tile = _pick(m, (2112, 2048, 1024, 528, 512, 256, 128, 64, 32, 16))
    sub = tile
    n_tiles = m // tile
    tile_expert = jnp.full((n_tiles,), which, jnp.int32)
    tile_nsub = jnp.full((n_tiles,), tile // sub, jnp.int32)
    tile_src = jnp.arange(n_tiles, dtype=jnp.int32)
    g = ffn_up(h, w1, w3, tile_expert, tile_nsub, tile_src, tile=tile, sub=sub)
    return ffn_down(g, w2, tile_expert, tile_nsub, tile_src, tile=tile, sub=sub, residual=x)


def kernel(x_prompt, x_sample, state_hgrn, state_conv, cache_k, cache_v, lb_param, g_mix, w_in, g_hgrn, w_out_a, w_dw, b_dw, ln_g, ln_b, w_out_b, g_qn, g_kn, sinks, w_out_c, w_o, g_ffn, w1_dense, w3_dense, w2_dense, w_router, w1_moe, w3_moe, w2_moe):
    bp, tp, d = x_prompt.shape
    bs, ts, _ = x_sample.shape
    depth = w_in.shape[0]
    heads_a = state_hgrn.shape[2]
    channels_b = state_conv.shape[-1]
    conv_w = w_dw.shape[1]
    kv_heads = cache_k.shape[3]
    q_heads = sinks.shape[1]
    n_prompt = bp * tp
    m_rows = n_prompt + bs * ts
    past_len = PAST_LEN

    a_cols = heads_a * HEAD_A
    lin_col = 4 * a_cols
    gate_b_col = lin_col + channels_b
    q_col = gate_b_col + channels_b
    k_col = q_col + q_heads * HEAD_C
    merge_gate_col = k_col + 2 * kv_heads * HEAD_C

    p_soft = jax.nn.softmax(lb_param.astype(F32), axis=0)
    lower = jnp.cumsum(p_soft, axis=0) - p_soft[0:1]

    x = jnp.concatenate([x_prompt.reshape(n_prompt, d), x_sample.reshape(bs * ts, d)], axis=0)
    zero_state = jnp.zeros((bp, heads_a, HEAD_A, HEAD_A), F32)
    zero_hist = jnp.zeros((bp, HIST_ROWS, channels_b), F32)
    outs = {name: [] for name in ("hp", "cp", "kp", "vp", "hs", "cs", "ks", "vs")}

    for l in range(depth):
        h = rmsnorm_rows(x, g_mix[l], BF16)
        proj = gemm(h, w_in, l)

        o_a, s_p = hgrn_mixer(proj, lower[l], g_hgrn[l], zero_state, batch=bp, seq=tp,
                              chunk=_pick(tp, (128, 64, 32, 16, 8)), row0=0, heads=heads_a, total_rows=m_rows,
                              prev_out=jnp.zeros((m_rows, a_cols), F32))
        o_a, s_s = hgrn_mixer(proj, lower[l], g_hgrn[l], state_hgrn[l], batch=bs, seq=ts,
                              chunk=_pick(ts, (128, 64, 32, 16, 8)), row0=n_prompt, heads=heads_a, total_rows=m_rows,
                              prev_out=o_a)

        hist_s = jnp.pad(state_conv[l], ((0, 0), (HIST_ROWS - (conv_w - 1), 0), (0, 0)))
        conv, c_p = conv_mixer(proj, zero_hist, w_dw[l], b_dw[l], batch=bp, seq=tp,
                               tb=_pick(tp, (128, 64, 32, 16, 8)), row0=0, lin_col=lin_col, gate_col=gate_b_col,
                               total_rows=m_rows, prev_out=jnp.zeros((m_rows, channels_b), F32))
        conv, c_s = conv_mixer(proj, hist_s, w_dw[l], b_dw[l], batch=bs, seq=ts,
                               tb=_pick(ts, (128, 64, 32, 16, 8)), row0=n_prompt, lin_col=lin_col,
                               gate_col=gate_b_col, total_rows=m_rows, prev_out=conv)
        o_b = layernorm_silu_rows(conv, ln_g[l], ln_b[l], BF16)

        o_c, k_p, v_p = attn_prompt(proj, sinks[l], g_qn[l], g_kn[l], jnp.zeros((m_rows, q_heads * HEAD_C), F32),
                                    batch=bp, seq=tp, q_col=q_col, k_col=k_col, q_heads=q_heads, kv_heads=kv_heads)
        o_c, k_s, v_s = attn_sample(proj, cache_k[l].reshape(bs, WINDOW, kv_heads * HEAD_C),
                                    cache_v[l].reshape(bs, WINDOW, kv_heads * HEAD_C), sinks[l], g_qn[l], g_kn[l],
                                    o_c, batch=bs, seq=ts, start=past_len, row0=n_prompt, q_col=q_col, k_col=k_col,
                                    q_heads=q_heads, kv_heads=kv_heads)

        mixed = gated_merge(o_a, o_b, o_c, w_out_a, w_out_b, w_out_c, proj, l, merge_gate_col, d)
        x = gemm(mixed, w_o, l, residual=x)

        j = l // 2
        if l % 2 == 0:
            x = dense_ffn(x, g_ffn[l], w1_dense, w3_dense, w2_dense, j)
        else:
            x = moe_ffn(x, g_ffn[l], w_router[j], w1_moe, w3_moe, w2_moe, j)

        outs["hp"].append(s_p)
        outs["cp"].append(c_p)
        outs["kp"].append(k_p.reshape(bp, WINDOW, kv_heads, HEAD_C))
        outs["vp"].append(v_p.reshape(bp, WINDOW, kv_heads, HEAD_C))
        outs["hs"].append(s_s)
        outs["cs"].append(c_s)
        outs["ks"].append(k_s.reshape(bs, WINDOW, kv_heads, HEAD_C))
        outs["vs"].append(v_s.reshape(bs, WINDOW, kv_heads, HEAD_C))

    y_prompt = x[:n_prompt].reshape(bp, tp, d)
    y_sample = x[n_prompt:].reshape(bs, ts, d)
    return (y_prompt, y_sample,
            jnp.stack(outs["hp"]), jnp.stack(outs["cp"]), jnp.stack(outs["kp"]), jnp.stack(outs["vp"]),
            jnp.stack(outs["hs"]), jnp.stack(outs["cs"]), jnp.stack(outs["ks"]), jnp.stack(outs["vs"]))
```

```python
import functools

import jax
import jax.numpy as jnp
from jax import lax
from jax.experimental import pallas as pl
from jax.experimental.pallas import tpu as pltpu

F32 = jnp.float32
BF16 = jnp.bfloat16

RMS_EPS = 1e-6
LN_EPS = 1e-5
F_MIN = 1e-30
NEG_BIG = -1e30
ROPE_THETA = 500000.0
HEAD_A = 128
HEAD_C = 64
WINDOW = 128
PAST_LEN = 16384
TOP_K = 2
HIGHEST = lax.Precision.HIGHEST

V7X_VMEM_LIMIT_BYTES = 56 * 1024 * 1024
LANES = 128
SUBLANES = 8


def _pick(n, candidates):
    for c in candidates:
        if n % c == 0:
            return c
    raise ValueError(f"no tile in {candidates} divides {n}")


def _params(*semantics):
    return pltpu.CompilerParams(dimension_semantics=semantics, vmem_limit_bytes=V7X_VMEM_LIMIT_BYTES)


def _silu(x):
    return x * jax.nn.sigmoid(x)


def _rmsnorm_kernel(x_ref, g_ref, o_ref):
    x = x_ref[...]
    ms = jnp.mean(x * x, axis=-1, keepdims=True)
    o_ref[...] = (x * lax.rsqrt(ms + RMS_EPS) * g_ref[...]).astype(o_ref.dtype)


def rmsnorm_rows(x, g, out_dtype):
    m, d = x.shape
    tm = _pick(m, (256, 128, 64, 32, 16))
    return pl.pallas_call(
        _rmsnorm_kernel,
        grid=(m // tm,),
        in_specs=[pl.BlockSpec((tm, d), lambda i: (i, 0)), pl.BlockSpec((1, d), lambda i: (0, 0))],
        out_specs=pl.BlockSpec((tm, d), lambda i: (i, 0)),
        out_shape=jax.ShapeDtypeStruct((m, d), out_dtype),
        compiler_params=_params("parallel"),
        name="rmsnorm_rows",
    )(x, g.reshape(1, d))


def _ln_silu_kernel(x_ref, g_ref, b_ref, o_ref):
    x = x_ref[...]
    mu = jnp.mean(x, axis=-1, keepdims=True)
    xc = x - mu
    var = jnp.mean(xc * xc, axis=-1, keepdims=True)
    y = xc * lax.rsqrt(var + LN_EPS) * g_ref[...] + b_ref[...]
    o_ref[...] = _silu(y).astype(o_ref.dtype)


def layernorm_silu_rows(x, g, b, out_dtype):
    m, d = x.shape
    tm = _pick(m, (256, 128, 64, 32, 16))
    return pl.pallas_call(
        _ln_silu_kernel,
        grid=(m // tm,),
        in_specs=[pl.BlockSpec((tm, d), lambda i: (i, 0)),
                  pl.BlockSpec((1, d), lambda i: (0, 0)),
                  pl.BlockSpec((1, d), lambda i: (0, 0))],
        out_specs=pl.BlockSpec((tm, d), lambda i: (i, 0)),
        out_shape=jax.ShapeDtypeStruct((m, d), out_dtype),
        compiler_params=_params("parallel"),
        name="layernorm_silu_rows",
    )(x, g.reshape(1, d), b.reshape(1, d))


def _gemm_kernel(x_ref, w_ref, o_ref):
    o_ref[...] = jnp.dot(x_ref[...].astype(BF16), w_ref[...].astype(BF16), preferred_element_type=F32)


def _gemm_residual_kernel(x_ref, w_ref, r_ref, o_ref):
    o_ref[...] = r_ref[...] + jnp.dot(x_ref[...].astype(BF16), w_ref[...].astype(BF16),
                                      preferred_element_type=F32)


def gemm(x, w, layer, residual=None):
    m, k = x.shape
    n = w.shape[-1]
    tm = _pick(m, (2112, 2048, 1024, 512, 256, 128, 64, 32, 16) if residual is None
               else (1056, 1024, 512, 256, 128, 64, 32, 16))
    tn = _pick(n, (512, 256, 128))
    in_specs = [pl.BlockSpec((tm, k), lambda i, j: (i, 0), pipeline_mode=pl.Buffered(1)),
                pl.BlockSpec((None, k, tn), lambda i, j: (layer, 0, j))]
    args = [x, w]
    body = _gemm_kernel
    if residual is not None:
        in_specs.append(pl.BlockSpec((tm, tn), lambda i, j: (i, j)))
        args.append(residual)
        body = _gemm_residual_kernel
    return pl.pallas_call(
        body,
        grid=(m // tm, n // tn),
        in_specs=in_specs,
        out_specs=pl.BlockSpec((tm, tn), lambda i, j: (i, j)),
        out_shape=jax.ShapeDtypeStruct((m, n), F32),
        compiler_params=_params("parallel", "arbitrary"),
        name="gemm_residual" if residual is not None else "gemm",
    )(*args)


def _merge_kernel(oa_ref, ob_ref, oc_ref, wa_ref, wb_ref, wc_ref, ga_ref, gb_ref, gc_ref, o_ref):
    def branch(o_ref_, w_ref_, g_ref_):
        y = jnp.dot(o_ref_[...].astype(BF16), w_ref_[...].astype(BF16), preferred_element_type=F32)
        return jax.nn.sigmoid(g_ref_[...]) * y

    mixed = branch(oa_ref, wa_ref, ga_ref) + branch(ob_ref, wb_ref, gb_ref) + branch(oc_ref, wc_ref, gc_ref)
    o_ref[...] = mixed.astype(o_ref.dtype)


def gated_merge(o_a, o_b, o_c, w_out_a, w_out_b, w_out_c, proj, layer, gate_col, d_model):
    m = o_a.shape[0]
    tm = _pick(m, (1056, 1024, 512, 256, 128, 64, 32, 16))
    tn = next(c for c in (256, 128) if d_model % c == 0 and gate_col % c == 0)
    gate_blk = gate_col // tn
    per_gate = d_model // tn

    def o_spec(o):
        return pl.BlockSpec((tm, o.shape[1]), lambda i, j: (i, 0), pipeline_mode=pl.Buffered(1))

    def w_spec(w):
        return pl.BlockSpec((None, w.shape[1], tn), lambda i, j: (layer, 0, j))

    def g_spec(which):
        return pl.BlockSpec((tm, tn), lambda i, j: (i, gate_blk + which * per_gate + j))

    return pl.pallas_call(
        _merge_kernel,
        grid=(m // tm, d_model // tn),
        in_specs=[o_spec(o_a), o_spec(o_b), o_spec(o_c), w_spec(w_out_a), w_spec(w_out_b), w_spec(w_out_c),
                  g_spec(0), g_spec(1), g_spec(2)],
        out_specs=pl.BlockSpec((tm, tn), lambda i, j: (i, j)),
        out_shape=jax.ShapeDtypeStruct((m, d_model), BF16),
        compiler_params=_params("parallel", "arbitrary"),
        name="gated_merge",
    )(o_a, o_b, o_c, w_out_a, w_out_b, w_out_c, proj, proj, proj)


def _cumsum_rows(x):
    n = x.shape[0]
    row = lax.broadcasted_iota(jnp.int32, x.shape, 0)
    shift = 1
    while shift < n:
        x = x + jnp.where(row >= shift, pltpu.roll(x, shift, axis=0), 0.0)
        shift *= 2
    return x


def _bf16_dot_t(a, b):
    return lax.dot_general(a.astype(BF16), b.astype(BF16), (((1,), (1,)), ((), ())), preferred_element_type=F32)


def _hgrn_head(q_in, f_in, v, gate_in, lb, gain, st, *, chunk):
    sig = jax.nn.sigmoid(f_in)
    f_gate = lb + (1.0 - lb) * sig
    log_f = jnp.log(jnp.maximum(f_gate, F_MIN))
    k = (1.0 - lb) * (1.0 - sig)
    q = _silu(q_in)
    b = _cumsum_rows(log_f)

    o = _bf16_dot_t(q * jnp.exp(b), st)

    n_sub = chunk // SUBLANES
    q3, k3, b3, v3 = (a.reshape(n_sub, SUBLANES, HEAD_A) for a in (q, k, b, v))
    r8 = lax.broadcasted_iota(jnp.int32, (n_sub, SUBLANES, 1), 1)
    acc = jnp.zeros((n_sub, SUBLANES, HEAD_A), F32)
    for s8 in range(SUBLANES):
        causal = r8 >= s8
        decay = jnp.exp(jnp.where(causal, b3 - b3[:, s8:s8 + 1, :], 0.0))
        score = jnp.sum(jnp.where(causal, q3 * k3[:, s8:s8 + 1, :] * decay, 0.0), axis=-1, keepdims=True)
        acc = acc + score * v3[:, s8:s8 + 1, :]
    o = o + acc.reshape(chunk, HEAD_A)

    row = lax.broadcasted_iota(jnp.int32, (chunk, 1), 0)
    scores = None
    m = SUBLANES
    while 2 * m <= chunk:
        n_blk = chunk // (2 * m)
        ref_row = b.reshape(n_blk, 2 * m, HEAD_A)[:, m - 1:m, :]
        beta = jnp.broadcast_to(ref_row, (n_blk, 2 * m, HEAD_A)).reshape(chunk, HEAD_A)
        second = lax.rem(row, 2 * m) >= m
        q_m = jnp.where(second, q * jnp.exp(jnp.where(second, b - beta, 0.0)), 0.0)
        k_m = jnp.where(second, 0.0, k * jnp.exp(jnp.where(second, 0.0, beta - b)))
        a = _bf16_dot_t(q_m, k_m)
        if n_blk > 1:
            blk_r = lax.broadcasted_iota(jnp.int32, (chunk, chunk), 0) // (2 * m)
            blk_c = lax.broadcasted_iota(jnp.int32, (chunk, chunk), 1) // (2 * m)
            a = jnp.where(blk_r == blk_c, a, 0.0)
        scores = a if scores is None else scores + a
        m *= 2
    if scores is not None:
        o = o + jnp.dot(scores.astype(BF16), v.astype(BF16), preferred_element_type=F32)

    o_n = o * lax.rsqrt(jnp.mean(o * o, axis=-1, keepdims=True) + RMS_EPS) * gain
    out = o_n * _silu(gate_in)

    b_last = b[chunk - 1:chunk, :]
    k_dec = k * jnp.exp(b_last - b)
    st_new = jnp.exp(b_last) * st + lax.dot_general(v.astype(BF16), k_dec.astype(BF16), (((0,), (0,)), ((), ())),
                                                    preferred_element_type=F32)
    return out, st_new


def _hgrn_kernel(q_ref, f_ref, i_ref, g_ref, lb_ref, gh_ref, s0_ref, *rest, chunk, n_chunks, heads_per_step,
                 aliased):
    o_ref, sfin_ref, st_ref = rest[1:] if aliased else rest
    c = pl.program_id(2)

    @pl.when(c == 0)
    def _():
        for hh in range(heads_per_step):
            st_ref[hh] = s0_ref[hh].T

    for hh in range(heads_per_step):
        cols = slice(hh * HEAD_A, (hh + 1) * HEAD_A)
        out, st_new = _hgrn_head(q_ref[:, cols], f_ref[:, cols], i_ref[:, cols], g_ref[:, cols], lb_ref[:, cols],
                                 gh_ref[...], st_ref[hh], chunk=chunk)
        o_ref[:, cols] = out
        st_ref[hh] = st_new

        @pl.when(c == n_chunks - 1)
        def _(hh=hh, st_new=st_new):
            sfin_ref[hh] = st_new.T


def hgrn_mixer(proj, lb, g_hgrn, s0, *, batch, seq, chunk, row0, heads, total_rows, prev_out=None):
    n_chunks = seq // chunk
    blk0 = row0 // chunk
    hps = _pick(heads, (8, 4, 2, 1))
    width = hps * HEAD_A

    def p_spec(slab):
        return pl.BlockSpec((chunk, width), lambda b, h, c: (blk0 + b * n_chunks + c, slab * (heads // hps) + h))

    in_specs = [p_spec(0), p_spec(1), p_spec(2), p_spec(3),
                pl.BlockSpec((1, width), lambda b, h, c: (0, h)),
                pl.BlockSpec((1, HEAD_A), lambda b, h, c: (0, 0)),
                pl.BlockSpec((None, hps, HEAD_A, HEAD_A), lambda b, h, c: (b, h, 0, 0))]
    args = [proj, proj, proj, proj, lb.reshape(1, heads * HEAD_A), g_hgrn.reshape(1, HEAD_A), s0]
    aliases = {}
    if prev_out is not None:
        in_specs.append(pl.BlockSpec(memory_space=pl.ANY))
        args.append(prev_out)
        aliases = {len(args) - 1: 0}
    kern = functools.partial(_hgrn_kernel, chunk=chunk, n_chunks=n_chunks, heads_per_step=hps,
                             aliased=prev_out is not None)
    return pl.pallas_call(
        kern,
        grid=(batch, heads // hps, n_chunks),
        in_specs=in_specs,
        out_specs=[pl.BlockSpec((chunk, width), lambda b, h, c: (blk0 + b * n_chunks + c, h)),
                   pl.BlockSpec((None, hps, HEAD_A, HEAD_A), lambda b, h, c: (b, h, 0, 0))],
        out_shape=[jax.ShapeDtypeStruct((total_rows, heads * HEAD_A), F32),
                   jax.ShapeDtypeStruct((batch, heads, HEAD_A, HEAD_A), F32)],
        scratch_shapes=[pltpu.VMEM((hps, HEAD_A, HEAD_A), F32)],
        input_output_aliases=aliases,
        compiler_params=_params("parallel", "parallel", "arbitrary"),
        name="hgrn_mixer",
    )(*args)


HIST_ROWS = 32


def _conv_kernel(ul_ref, ug_ref, hist_ref, w_ref, bias_ref, *rest, tb, conv_w, aliased):
    o_ref, state_ref, buf_ref, shifted_ref = rest[1:] if aliased else rest
    t = pl.program_id(2)

    @pl.when(t == 0)
    def _():
        buf_ref[0:HIST_ROWS, :] = hist_ref[...]

    u = ul_ref[...] * jax.nn.sigmoid(ug_ref[...])
    buf_ref[HIST_ROWS:HIST_ROWS + tb, :] = u
    first = HIST_ROWS - (conv_w - 1)
    span = tb + HIST_ROWS - SUBLANES
    for phase in range(1, SUBLANES):
        shifted_ref[phase, 0:span, :] = buf_ref[phase:phase + span, :]
    acc = jnp.zeros(u.shape, F32)
    for j in range(conv_w):
        phase = (first + j) % SUBLANES
        start = first + j - phase
        rows = buf_ref[start:start + tb, :] if phase == 0 else shifted_ref[phase, start:start + tb, :]
        acc = acc + w_ref[j:j + 1, :] * rows
    o_ref[...] = acc + bias_ref[...]
    state_ref[...] = buf_ref[tb + first:tb + HIST_ROWS, :]
    tail = buf_ref[tb:tb + HIST_ROWS, :]
    buf_ref[0:HIST_ROWS, :] = tail


def conv_mixer(proj, hist, w_dw, b_dw, *, batch, seq, tb, row0, lin_col, gate_col, total_rows, prev_out=None):
    conv_w, channels = w_dw.shape
    cw = _pick(channels, (256, 128))
    n_t = seq // tb
    blk0 = row0 // tb
    in_specs = [pl.BlockSpec((tb, cw), lambda b, c, t: (blk0 + b * n_t + t, lin_col // cw + c)),
                pl.BlockSpec((tb, cw), lambda b, c, t: (blk0 + b * n_t + t, gate_col // cw + c)),
                pl.BlockSpec((None, HIST_ROWS, cw), lambda b, c, t: (b, 0, c)),
                pl.BlockSpec((conv_w, cw), lambda b, c, t: (0, c)),
                pl.BlockSpec((1, cw), lambda b, c, t: (0, c))]
    args = [proj, proj, hist, w_dw, b_dw.reshape(1, channels)]
    aliases = {}
    if prev_out is not None:
        in_specs.append(pl.BlockSpec(memory_space=pl.ANY))
        args.append(prev_out)
        aliases = {len(args) - 1: 0}
    kern = functools.partial(_conv_kernel, tb=tb, conv_w=conv_w, aliased=prev_out is not None)
    return pl.pallas_call(
        kern,
        grid=(batch, channels // cw, n_t),
        in_specs=in_specs,
        out_specs=[pl.BlockSpec((tb, cw), lambda b, c, t: (blk0 + b * n_t + t, c)),
                   pl.BlockSpec((None, conv_w - 1, cw), lambda b, c, t: (b, 0, c))],
        out_shape=[jax.ShapeDtypeStruct((total_rows, channels), F32),
                   jax.ShapeDtypeStruct((batch, conv_w - 1, channels), F32)],
        scratch_shapes=[pltpu.VMEM((HIST_ROWS + tb, cw), F32),
                        pltpu.VMEM((SUBLANES, HIST_ROWS + tb - SUBLANES, cw), F32)],
        input_output_aliases=aliases,
        compiler_params=_params("parallel", "parallel", "arbitrary"),
        name="conv_mixer",
    )(*args)


def _rope_tables(pos):
    rot = HEAD_C // 4
    inv_freq = jnp.float32(ROPE_THETA) ** (-jnp.arange(0, rot, 2, dtype=F32) / rot)
    ang = pos.astype(F32)[:, None] * inv_freq[None, :]
    cos, sin = jnp.cos(ang), jnp.sin(ang)
    rest = HEAD_C - rot
    c_tab = jnp.concatenate([cos, cos, jnp.ones((pos.shape[0], rest), F32)], axis=-1)
    s_tab = jnp.concatenate([-sin, sin, jnp.zeros((pos.shape[0], rest), F32)], axis=-1)
    return c_tab, s_tab


def _norm_rope(x, gain, c_tab, s_tab):
    half = HEAD_C // 8
    y = x * lax.rsqrt(jnp.mean(x * x, axis=-1, keepdims=True) + RMS_EPS) * gain
    swapped = jnp.concatenate([y[:, half:2 * half], y[:, 0:half], y[:, 2 * half:]], axis=-1)
    return y * c_tab + swapped * s_tab


PAIR = 2 * HEAD_C


def _norm_rope_pair(x, gain, c_tab, s_tab, lane):
    half = HEAD_C // 8
    low = lane < HEAD_C
    sq = x * x
    ss_low = jnp.sum(jnp.where(low, sq, 0.0), axis=-1, keepdims=True)
    ss_high = jnp.sum(jnp.where(low, 0.0, sq), axis=-1, keepdims=True)
    ms = jnp.where(low, ss_low, ss_high) * (1.0 / HEAD_C)
    y = x * lax.rsqrt(ms + RMS_EPS) * gain
    in_head = lane & (HEAD_C - 1)
    ahead = pltpu.roll(y, PAIR - half, axis=1)
    behind = pltpu.roll(y, half, axis=1)
    swapped = jnp.where(in_head < half, ahead, jnp.where(in_head < 2 * half, behind, y))
    return y * c_tab + swapped * s_tab


def _softmax_sink(logits, valid, sink):
    logits = jnp.where(valid, logits, NEG_BIG)
    m = jnp.maximum(jnp.max(logits, axis=-1, keepdims=True), sink)
    p = jnp.where(valid, jnp.exp(logits - m), 0.0)
    denom = jnp.sum(p, axis=-1, keepdims=True) + jnp.exp(sink - m)
    return p / denom


def _qk_logits(q, k):
    return lax.dot_general(q.astype(BF16), k.astype(BF16), (((1,), (1,)), ((), ())),
                           preferred_element_type=F32) * (HEAD_C ** -0.5)


def _pv(p, v):
    return jnp.dot(p.astype(BF16), v.astype(BF16), preferred_element_type=F32)


def _attn_prompt_kernel(sinks_ref, q_ref, kv_ref, c_ref, s_ref, gq_ref, gk_ref, prev_ref, o_ref, newk_ref, newv_ref,
                        kprev_ref, vprev_ref, *, kv_heads, group):
    del prev_ref
    j = pl.program_id(1)
    kvw = kv_heads * HEAD_C

    @pl.when(j == 0)
    def _():
        kprev_ref[...] = jnp.zeros_like(kprev_ref)
        vprev_ref[...] = jnp.zeros_like(vprev_ref)

    c_tab = c_ref[...]
    s_tab = s_ref[...]
    gain_q = gq_ref[...]
    gain_k = gk_ref[...]
    lane = lax.broadcasted_iota(jnp.int32, (1, PAIR), 1)
    low = lane < HEAD_C
    slabs_per_kv = group // 2
    rows = slabs_per_kv * WINDOW
    row = lax.broadcasted_iota(jnp.int32, (rows, 2 * WINDOW), 0)
    cidx = lax.broadcasted_iota(jnp.int32, (rows, 2 * WINDOW), 1)
    r = lax.rem(row, WINDOW)
    first_key = WINDOW - jnp.minimum(j, 1) * WINDOW
    valid = (cidx > r) & (cidx <= r + WINDOW) & (cidx >= first_key)
    slab_of_row = lax.broadcasted_iota(jnp.int32, (rows, 1), 0) // WINDOW

    def block_diag(own_half_is_low, x):
        own = jnp.where(low if own_half_is_low else ~low, x, 0.0)
        other = pltpu.roll(own, HEAD_C, axis=1)
        top, bottom = (own, other) if own_half_is_low else (other, own)
        return jnp.concatenate([top, bottom], axis=0)

    for kv_slab in range(kv_heads // 2):
        lanes = slice(kv_slab * PAIR, (kv_slab + 1) * PAIR)
        k_new = _norm_rope_pair(kv_ref[:, lanes], gain_k, c_tab, s_tab, lane)
        v_new = kv_ref[:, kvw + kv_slab * PAIR:kvw + (kv_slab + 1) * PAIR]
        keys2 = jnp.concatenate([kprev_ref[kv_slab], k_new], axis=0)
        vals2 = jnp.concatenate([vprev_ref[kv_slab], v_new], axis=0)
        for half in range(2):
            g = 2 * kv_slab + half
            k_blk = block_diag(half == 0, keys2)
            v_blk = block_diag(half == 0, vals2)
            slabs = [g * slabs_per_kv + i for i in range(slabs_per_kv)]
            q_g = jnp.concatenate([_norm_rope_pair(q_ref[:, s * PAIR:(s + 1) * PAIR], gain_q, c_tab, s_tab, lane)
                                   for s in slabs], axis=0)
            logits = _qk_logits(q_g, k_blk)
            probs = []
            for odd in range(2):
                sink = jnp.zeros((rows, 1), F32)
                for i, s in enumerate(slabs):
                    sink = jnp.where(slab_of_row == i, sinks_ref[2 * s + odd], sink)
                probs.append(_softmax_sink(logits[:, odd * 2 * WINDOW:(odd + 1) * 2 * WINDOW], valid, sink))
            o_g = _pv(jnp.concatenate(probs, axis=1), v_blk)
            for i, s in enumerate(slabs):
                o_ref[:, s * PAIR:(s + 1) * PAIR] = o_g[i * WINDOW:(i + 1) * WINDOW, :]
        kprev_ref[kv_slab] = k_new
        vprev_ref[kv_slab] = v_new
        newk_ref[:, lanes] = k_new
        newv_ref[:, lanes] = v_new


def attn_prompt(proj, sinks, g_qn, g_kn, prev_out, *, batch, seq, q_col, k_col, q_heads, kv_heads):
    nb = seq // WINDOW
    qw = q_heads * HEAD_C
    kvw = kv_heads * HEAD_C
    if kv_heads % 2 or (q_heads // kv_heads) % 2:
        raise ValueError("attn_prompt pairs adjacent heads: kv_heads and the group size must be even")
    pair = lambda a: jnp.tile(a, (1, 2))
    c_tab, s_tab = (pair(t) for t in _rope_tables(jnp.arange(seq)))
    kern = functools.partial(_attn_prompt_kernel, kv_heads=kv_heads, group=q_heads // kv_heads)
    return pl.pallas_call(
        kern,
        grid=(batch, nb),
        in_specs=[pl.BlockSpec(memory_space=pltpu.SMEM),
                  pl.BlockSpec((WINDOW, qw), lambda b, j: (b * nb + j, q_col // qw)),
                  pl.BlockSpec((WINDOW, 2 * kvw), lambda b, j: (b * nb + j, k_col // (2 * kvw))),
                  pl.BlockSpec((WINDOW, PAIR), lambda b, j: (j, 0)),
                  pl.BlockSpec((WINDOW, PAIR), lambda b, j: (j, 0)),
                  pl.BlockSpec((1, PAIR), lambda b, j: (0, 0)),
                  pl.BlockSpec((1, PAIR), lambda b, j: (0, 0)),
                  pl.BlockSpec(memory_space=pl.ANY)],
        out_specs=[pl.BlockSpec((WINDOW, qw), lambda b, j: (b * nb + j, 0)),
                   pl.BlockSpec((None, WINDOW, kvw), lambda b, j: (b, 0, 0)),
                   pl.BlockSpec((None, WINDOW, kvw), lambda b, j: (b, 0, 0))],
        out_shape=[jax.ShapeDtypeStruct(prev_out.shape, F32),
                   jax.ShapeDtypeStruct((batch, WINDOW, kvw), F32),
                   jax.ShapeDtypeStruct((batch, WINDOW, kvw), F32)],
        scratch_shapes=[pltpu.VMEM((kv_heads // 2, WINDOW, PAIR), F32), pltpu.VMEM((kv_heads // 2, WINDOW, PAIR), F32)],
        input_output_aliases={7: 0},
        compiler_params=_params("parallel", "arbitrary"),
        name="attn_prompt",
    )(sinks, proj, proj, c_tab, s_tab, pair(g_qn.reshape(1, HEAD_C)), pair(g_kn.reshape(1, HEAD_C)), prev_out)


def _attn_sample_kernel(sinks_ref, q_ref, kv_ref, ck_ref, cv_ref, c_ref, s_ref, gq_ref, gk_ref, prev_ref,
                        o_ref, newk_ref, newv_ref, *, kv_heads, group, t_new):
    del prev_ref
    kvw = kv_heads * HEAD_C
    c_tab = c_ref[...]
    s_tab = s_ref[...]
    rows = group * t_new
    row = lax.broadcasted_iota(jnp.int32, (rows, 2 * WINDOW), 0)
    cidx = lax.broadcasted_iota(jnp.int32, (rows, 2 * WINDOW), 1)
    r = lax.rem(row, t_new)
    valid = ((cidx < WINDOW) & (cidx > r)) | ((cidx >= WINDOW) & (cidx - WINDOW <= r))
    head_in_group = lax.broadcasted_iota(jnp.int32, (rows, 1), 0) // t_new
    filler = jnp.zeros((WINDOW - t_new, HEAD_C), F32)

    for g in range(kv_heads):
        k_new = _norm_rope(kv_ref[:, g * HEAD_C:(g + 1) * HEAD_C], gk_ref[...], c_tab, s_tab)
        v_new = kv_ref[:, kvw + g * HEAD_C:kvw + (g + 1) * HEAD_C]
        k_old = ck_ref[:, g * HEAD_C:(g + 1) * HEAD_C]
        v_old = cv_ref[:, g * HEAD_C:(g + 1) * HEAD_C]
        keys = jnp.concatenate([k_old, k_new, filler], axis=0)
        vals = jnp.concatenate([v_old, v_new, filler], axis=0)
        heads = [g * group + i for i in range(group)]
        q_g = jnp.concatenate([_norm_rope(q_ref[:, h * HEAD_C:(h + 1) * HEAD_C], gq_ref[...], c_tab, s_tab)
                               for h in heads], axis=0)
        sink = jnp.zeros((rows, 1), F32)
        for i, h in enumerate(heads):
            sink = jnp.where(head_in_group == i, sinks_ref[h], sink)
        o_g = _pv(_softmax_sink(_qk_logits(q_g, keys), valid, sink), vals)
        for i, h in enumerate(heads):
            o_ref[:, h * HEAD_C:(h + 1) * HEAD_C] = o_g[i * t_new:(i + 1) * t_new, :]
        newk_ref[0:WINDOW - t_new, g * HEAD_C:(g + 1) * HEAD_C] = k_old[t_new:, :]
        newk_ref[WINDOW - t_new:, g * HEAD_C:(g + 1) * HEAD_C] = k_new
        newv_ref[0:WINDOW - t_new, g * HEAD_C:(g + 1) * HEAD_C] = v_old[t_new:, :]
        newv_ref[WINDOW - t_new:, g * HEAD_C:(g + 1) * HEAD_C] = v_new


def attn_sample(proj, cache_k, cache_v, sinks, g_qn, g_kn, prev_out, *, batch, seq, start, row0, q_col, k_col,
                q_heads, kv_heads):
    qw = q_heads * HEAD_C
    kvw = kv_heads * HEAD_C
    c_tab, s_tab = _rope_tables(start + jnp.arange(seq))
    kern = functools.partial(_attn_sample_kernel, kv_heads=kv_heads, group=q_heads // kv_heads, t_new=seq)
    blk0 = row0 // seq
    return pl.pallas_call(
        kern,
        grid=(batch,),
        in_specs=[pl.BlockSpec(memory_space=pltpu.SMEM),
                  pl.BlockSpec((seq, qw), lambda b: (blk0 + b, q_col // qw)),
                  pl.BlockSpec((seq, 2 * kvw), lambda b: (blk0 + b, k_col // (2 * kvw))),
                  pl.BlockSpec((None, WINDOW, kvw), lambda b: (b, 0, 0)),
                  pl.BlockSpec((None, WINDOW, kvw), lambda b: (b, 0, 0)),
                  pl.BlockSpec((seq, HEAD_C), lambda b: (0, 0)),
                  pl.BlockSpec((seq, HEAD_C), lambda b: (0, 0)),
                  pl.BlockSpec((1, HEAD_C), lambda b: (0, 0)),
                  pl.BlockSpec((1, HEAD_C), lambda b: (0, 0)),
                  pl.BlockSpec(memory_space=pl.ANY)],
        out_specs=[pl.BlockSpec((seq, qw), lambda b: (blk0 + b, 0)),
                   pl.BlockSpec((None, WINDOW, kvw), lambda b: (b, 0, 0)),
                   pl.BlockSpec((None, WINDOW, kvw), lambda b: (b, 0, 0))],
        out_shape=[jax.ShapeDtypeStruct(prev_out.shape, F32),
                   jax.ShapeDtypeStruct((batch, WINDOW, kvw), F32),
                   jax.ShapeDtypeStruct((batch, WINDOW, kvw), F32)],
        input_output_aliases={9: 0},
        compiler_params=_params("parallel"),
        name="attn_sample",
    )(sinks, proj, proj, cache_k, cache_v, c_tab, s_tab, g_qn.reshape(1, HEAD_C), g_kn.reshape(1, HEAD_C),
      prev_out)


def _ffn_up_kernel(te_ref, ns_ref, ts_ref, x_ref, w1_ref, w3_ref, o_ref, w1b_ref, w3b_ref, *, sub, n_sub):
    n_valid = ns_ref[pl.program_id(0)]

    @pl.when(n_valid == n_sub)
    def _():
        x = x_ref[...]
        a = jnp.dot(x, w1_ref[...].astype(BF16), preferred_element_type=F32)
        b = jnp.dot(x, w3_ref[...].astype(BF16), preferred_element_type=F32)
        o_ref[...] = (_silu(a) * b).astype(o_ref.dtype)

    @pl.when((n_valid > 0) & (n_valid < n_sub))
    def _():
        w1b_ref[...] = w1_ref[...].astype(BF16)
        w3b_ref[...] = w3_ref[...].astype(BF16)

        def compute(s, carry):
            rows = pl.ds(pl.multiple_of(s * sub, sub), sub)
            x = x_ref[rows, :]
            a = jnp.dot(x, w1b_ref[...], preferred_element_type=F32)
            b = jnp.dot(x, w3b_ref[...], preferred_element_type=F32)
            o_ref[rows, :] = (_silu(a) * b).astype(o_ref.dtype)
            return carry

        def clear(s, carry):
            rows = pl.ds(pl.multiple_of(s * sub, sub), sub)
            o_ref[rows, :] = jnp.zeros((sub, o_ref.shape[1]), o_ref.dtype)
            return carry

        lax.fori_loop(0, n_valid, compute, 0)
        lax.fori_loop(n_valid, n_sub, clear, 0)

    @pl.when(n_valid == 0)
    def _():
        o_ref[...] = jnp.zeros_like(o_ref)


def ffn_up(x, w1, w3, tile_expert, tile_nsub, tile_src, *, tile, sub):
    m, k = x.shape
    f = w1.shape[-1]
    tf = _pick(f, (256, 128))
    nf = f // tf

    def col(i, j, ns):
        return jnp.where(ns[i] > 0, j, nf - 1)

    return pl.pallas_call(
        functools.partial(_ffn_up_kernel, sub=sub, n_sub=tile // sub),
        grid_spec=pltpu.PrefetchScalarGridSpec(
            num_scalar_prefetch=3,
            grid=(m // tile, nf),
            in_specs=[pl.BlockSpec((tile, k), lambda i, j, te, ns, ts: (ts[i], 0), pipeline_mode=pl.Buffered(1)),
                      pl.BlockSpec((None, k, tf), lambda i, j, te, ns, ts: (te[i], 0, col(i, j, ns))),
                      pl.BlockSpec((None, k, tf), lambda i, j, te, ns, ts: (te[i], 0, col(i, j, ns)))],
            out_specs=pl.BlockSpec((tile, tf), lambda i, j, te, ns, ts: (i, j)),
            scratch_shapes=[pltpu.VMEM((k, tf), BF16), pltpu.VMEM((k, tf), BF16)]),
        out_shape=jax.ShapeDtypeStruct((m, f), BF16),
        compiler_params=_params("arbitrary", "arbitrary"),
        name="ffn_up",
    )(tile_expert, tile_nsub, tile_src, x, w1, w3)


def _ffn_down_kernel(te_ref, ns_ref, ts_ref, g_ref, w2_ref, *rest, sub, n_sub, with_residual):
    if with_residual:
        r_ref, o_ref, wb_ref = rest
    else:
        o_ref, wb_ref = rest
    n_valid = ns_ref[pl.program_id(0)]

    @pl.when((n_valid > 0) & (pl.program_id(2) == 0))
    def _():
        o_ref[...] = r_ref[...] if with_residual else jnp.zeros_like(o_ref)

    @pl.when(n_valid == n_sub)
    def _():
        o_ref[...] += jnp.dot(g_ref[...], w2_ref[...].astype(BF16), preferred_element_type=F32)

    @pl.when((n_valid > 0) & (n_valid < n_sub))
    def _():
        wb_ref[...] = w2_ref[...].astype(BF16)

        def compute(s, carry):
            rows = pl.ds(pl.multiple_of(s * sub, sub), sub)
            o_ref[rows, :] += jnp.dot(g_ref[rows, :], wb_ref[...], preferred_element_type=F32)
            return carry

        lax.fori_loop(0, n_valid, compute, 0)

    @pl.when((n_valid == 0) & (pl.program_id(2) == 0))
    def _():
        o_ref[...] = jnp.zeros_like(o_ref)


def ffn_down(g, w2, tile_expert, tile_nsub, tile_src, *, tile, sub, residual=None):
    m, f = g.shape
    n = w2.shape[-1]
    tn = _pick(n, (1024, 512, 256, 128))
    tk = _pick(f, (1024, 896, 512, 256, 128))
    n_j, n_k = n // tn, f // tk

    def col(i, j, ns):
        return jnp.where(ns[i] > 0, j, n_j - 1)

    def red(i, kk, ns):
        return jnp.where(ns[i] > 0, kk, n_k - 1)

    in_specs = [pl.BlockSpec((tile, tk), lambda i, j, kk, te, ns, ts: (ts[i], red(i, kk, ns))),
                pl.BlockSpec((None, tk, tn), lambda i, j, kk, te, ns, ts: (te[i], red(i, kk, ns), col(i, j, ns)))]
    args = [tile_expert, tile_nsub, tile_src, g, w2]
    if residual is not None:
        in_specs.append(pl.BlockSpec((tile, tn), lambda i, j, kk, te, ns, ts: (ts[i], col(i, j, ns)),
                                     pipeline_mode=pl.Buffered(1)))
        args.append(residual)
    return pl.pallas_call(
        functools.partial(_ffn_down_kernel, sub=sub, n_sub=tile // sub, with_residual=residual is not None),
        grid_spec=pltpu.PrefetchScalarGridSpec(
            num_scalar_prefetch=3,
            grid=(m // tile, n_j, n_k),
            in_specs=in_specs,
            out_specs=pl.BlockSpec((tile, tn), lambda i, j, kk, te, ns, ts: (i, j)),
            scratch_shapes=[pltpu.VMEM((tk, tn), BF16)]),
        out_shape=jax.ShapeDtypeStruct((m, n), F32),
        compiler_params=_params("arbitrary", "arbitrary", "arbitrary"),
        name="ffn_down",
    )(*args)


def _router_kernel(x_ref, g_ref, w_ref, h_ref, idx_ref, gate_ref, *, n_experts):
    x = x_ref[...]
    h = x * lax.rsqrt(jnp.mean(x * x, axis=-1, keepdims=True) + RMS_EPS) * g_ref[...]
    h_ref[...] = h
    logits = jnp.dot(h, w_ref[...], precision=HIGHEST, preferred_element_type=F32)
    lane = lax.broadcasted_iota(jnp.int32, logits.shape, 1)
    logits = jnp.where(lane < n_experts, logits, -jnp.inf)
    v1 = jnp.max(logits, axis=-1, keepdims=True)
    i1 = jnp.min(jnp.where(logits == v1, lane, LANES), axis=-1, keepdims=True)
    rest = jnp.where(lane == i1, -jnp.inf, logits)
    v2 = jnp.max(rest, axis=-1, keepdims=True)
    i2 = jnp.min(jnp.where(rest == v2, lane, LANES), axis=-1, keepdims=True)
    e2 = jnp.exp(v2 - v1)
    g1 = 1.0 / (1.0 + e2)
    g2 = e2 / (1.0 + e2)
    idx_ref[...] = jnp.where(lane == 0, i1, jnp.where(lane == 1, i2, 0))
    gate_ref[...] = jnp.where(lane == 0, g1, jnp.where(lane == 1, g2, 0.0))


def router(x, g_ffn, w_router):
    m, d = x.shape
    n_experts = w_router.shape[-1]
    tm = _pick(m, (256, 128, 64, 32, 16))
    w_pad = jnp.zeros((d, LANES), F32).at[:, :n_experts].set(w_router)
    kern = functools.partial(_router_kernel, n_experts=n_experts)
    h, idx, gate = pl.pallas_call(
        kern,
        grid=(m // tm,),
        in_specs=[pl.BlockSpec((tm, d), lambda i: (i, 0)),
                  pl.BlockSpec((1, d), lambda i: (0, 0)),
                  pl.BlockSpec((d, LANES), lambda i: (0, 0))],
        out_specs=[pl.BlockSpec((tm, d), lambda i: (i, 0)),
                   pl.BlockSpec((tm, LANES), lambda i: (i, 0)),
                   pl.BlockSpec((tm, LANES), lambda i: (i, 0))],
        out_shape=[jax.ShapeDtypeStruct((m, d), F32),
                   jax.ShapeDtypeStruct((m, LANES), jnp.int32),
                   jax.ShapeDtypeStruct((m, LANES), F32)],
        compiler_params=_params("parallel"),
        name="router",
    )(x, g_ffn.reshape(1, d), w_pad)
    return h, idx[:, :TOP_K], gate[:, :TOP_K]


DMA_PRIORITIES = 2


def _row_copy(src_ref, dst_ref, sem, src_row, dst_row):
    return pltpu.make_async_copy(src_ref.at[pl.ds(src_row, 1)], dst_ref.at[pl.ds(dst_row, 1)], sem)


def _dispatch_kernel(idx_ref, valid_ref, src_ref, o_ref, buf_ref, sem, *, rows):
    i = pl.program_id(0)

    @pl.when(valid_ref[i] > 0)
    def _():
        base = i * rows

        def start(pair, carry):
            for priority in range(DMA_PRIORITIES):
                r = DMA_PRIORITIES * pair + priority
                _row_copy(src_ref, buf_ref, sem, idx_ref[base + r], r).start(priority=priority)
            return carry

        def wait(r, carry):
            _row_copy(src_ref, buf_ref, sem, idx_ref[base + r], r).wait()
            return carry

        lax.fori_loop(0, rows // DMA_PRIORITIES, start, 0)
        lax.fori_loop(0, rows, wait, 0)
        o_ref[...] = buf_ref[...].astype(o_ref.dtype)

    @pl.when(valid_ref[i] == 0)
    def _():
        o_ref[...] = jnp.zeros_like(o_ref)


def dispatch_rows(src, idx, block_valid, rows):
    n = idx.shape[0]
    d = src.shape[1]
    return pl.pallas_call(
        functools.partial(_dispatch_kernel, rows=rows),
        grid_spec=pltpu.PrefetchScalarGridSpec(
            num_scalar_prefetch=2,
            grid=(n // rows,),
            in_specs=[pl.BlockSpec(memory_space=pl.ANY)],
            out_specs=pl.BlockSpec((rows, d), lambda i, idx, valid: (i, 0)),
            scratch_shapes=[pltpu.VMEM((rows, d), src.dtype), pltpu.SemaphoreType.DMA(())]),
        out_shape=jax.ShapeDtypeStruct((n, d), BF16),
        compiler_params=_params("arbitrary"),
        name="dispatch_rows",
    )(idx, block_valid, src)


def _combine_kernel(slot_ref, x_ref, gate_ref, ys_ref, o_ref, ya_ref, yb_ref, sems, *, rows):
    base = pl.program_id(0) * rows

    def copies(r):
        token = base + r
        return (_row_copy(ys_ref, ya_ref, sems.at[0], slot_ref[TOP_K * token], r),
                _row_copy(ys_ref, yb_ref, sems.at[1], slot_ref[TOP_K * token + 1], r))

    def start(r, carry):
        for priority, cp in enumerate(copies(r)):
            cp.start(priority=priority)
        return carry

    def wait(r, carry):
        for cp in copies(r):
            cp.wait()
        return carry

    lax.fori_loop(0, rows, start, 0)
    lax.fori_loop(0, rows, wait, 0)
    gate = gate_ref[...]
    o_ref[...] = x_ref[...] + (gate[:, 0:1] * ya_ref[...] + gate[:, 1:2] * yb_ref[...])


def moe_combine(x, ys, slot, gate):
    m, d = x.shape
    rows = _pick(m, (256, 128, 64, 32, 16, 8))
    gate_pad = jnp.zeros((m, LANES), F32).at[:, :TOP_K].set(gate)
    return pl.pallas_call(
        functools.partial(_combine_kernel, rows=rows),
        grid_spec=pltpu.PrefetchScalarGridSpec(
            num_scalar_prefetch=1,
            grid=(m // rows,),
            in_specs=[pl.BlockSpec((rows, d), lambda i, slot: (i, 0)),
                      pl.BlockSpec((rows, LANES), lambda i, slot: (i, 0)),
                      pl.BlockSpec(memory_space=pl.ANY)],
            out_specs=pl.BlockSpec((rows, d), lambda i, slot: (i, 0)),
            scratch_shapes=[pltpu.VMEM((rows, d), F32), pltpu.VMEM((rows, d), F32),
                            pltpu.SemaphoreType.DMA((TOP_K,))]),
        out_shape=jax.ShapeDtypeStruct((m, d), F32),
        compiler_params=_params("arbitrary"),
        name="moe_combine",
    )(slot, x, gate_pad, ys)


MOE_SUB_ROWS = 256
MOE_TILE_HEADROOM = 1.05


def moe_ffn(x, g_ffn, w_router, w1, w3, w2, which):
    m, d = x.shape
    n_experts = w_router.shape[-1]
    w1, w3, w2 = (w.reshape((-1,) + w.shape[2:]) for w in (w1, w3, w2))
    h, top_i, top_w = router(x, g_ffn, w_router)

    n_assign = m * TOP_K
    sub = MOE_SUB_ROWS if n_assign >= 32 * MOE_SUB_ROWS else 32
    n_sub = -(-int(MOE_TILE_HEADROOM * n_assign / n_experts) // sub)
    tile = n_sub * sub
    n_tiles = (n_assign + n_experts * (tile - 1)) // tile
    flat_e = top_i.reshape(-1)
    counts = jnp.sum(jax.nn.one_hot(flat_e, n_experts, dtype=jnp.int32), axis=0)
    subs_per = (counts + sub - 1) // sub
    tiles_per = (subs_per + n_sub - 1) // n_sub
    tile_end = jnp.cumsum(tiles_per)
    tile_begin = tile_end - tiles_per
    order = jnp.argsort(flat_e, stable=True)
    sorted_e = flat_e[order]
    rank = jnp.arange(n_assign, dtype=jnp.int32) - (jnp.cumsum(counts) - counts)[sorted_e]
    slot_sorted = tile_begin[sorted_e] * tile + rank
    slot = jnp.zeros((n_assign,), jnp.int32).at[order].set(slot_sorted)
    row_token = jnp.zeros((n_tiles * tile,), jnp.int32).at[slot_sorted].set((order // TOP_K).astype(jnp.int32))
    tile_ids = jnp.arange(n_tiles, dtype=jnp.int32)
    n_used = tile_end[-1]
    owner = jnp.minimum(jnp.searchsorted(tile_end, tile_ids, side="right"), n_experts - 1).astype(jnp.int32)
    tile_nsub = jnp.clip(subs_per[owner] - (tile_ids - tile_begin[owner]) * n_sub, 0, n_sub)
    tile_nsub = jnp.where(tile_ids < n_used, tile_nsub, 0).astype(jnp.int32)
    tile_src = jnp.minimum(tile_ids, n_used - 1).astype(jnp.int32)
    tile_expert = owner[tile_src] + which * n_experts
    sub_valid = (jnp.arange(n_sub, dtype=jnp.int32)[None, :] < tile_nsub[:, None]).astype(jnp.int32).reshape(-1)

    xs = dispatch_rows(h, row_token, sub_valid, sub)
    g = ffn_up(xs, w1, w3, tile_expert, tile_nsub, tile_src, tile=tile, sub=sub)
    ys = ffn_down(g, w2, tile_expert, tile_nsub, tile_src, tile=tile, sub=sub)
    return moe_combine(x, ys, slot, top_w)


def dense_ffn(x, g_ffn, w1, w3, w2, which):
    m = x.shape[0]
    h = rmsnorm_rows(x, g_ffn, BF16)
    tile = _pick(m, (2112, 2048, 1024, 528, 512, 256, 128, 64, 32, 16))
    sub = tile
    n_tiles = m // tile
    tile_expert = jnp.full((n_tiles,), which, jnp.int32)
    tile_nsub = jnp.full((n_tiles,), tile // sub, jnp.int32)
    tile_src = jnp.arange(n_tiles, dtype=jnp.int32)
    g = ffn_up(h, w1, w3, tile_expert, tile_nsub, tile_src, tile=tile, sub=sub)
    return ffn_down(g, w2, tile_expert, tile_nsub, tile_src, tile=tile, sub=sub, residual=x)


def kernel(x_prompt, x_sample, state_hgrn, state_conv, cache_k, cache_v, lb_param, g_mix, w_in, g_hgrn, w_out_a, w_dw, b_dw, ln_g, ln_b, w_out_b, g_qn, g_kn, sinks, w_out_c, w_o, g_ffn, w1_dense, w3_dense, w2_dense, w_router, w1_moe, w3_moe, w2_moe):
    bp, tp, d = x_prompt.shape
    bs, ts, _ = x_sample.shape
    depth = w_in.shape[0]
    heads_a = state_hgrn.shape[2]
    channels_b = state_conv.shape[-1]
    conv_w = w_dw.shape[1]
    kv_heads = cache_k.shape[3]
    q_heads = sinks.shape[1]
    n_prompt = bp * tp
    m_rows = n_prompt + bs * ts
    past_len = PAST_LEN

    a_cols = heads_a * HEAD_A
    lin_col = 4 * a_cols
    gate_b_col = lin_col + channels_b
    q_col = gate_b_col + channels_b
    k_col = q_col + q_heads * HEAD_C
    merge_gate_col = k_col + 2 * kv_heads * HEAD_C

    p_soft = jax.nn.softmax(lb_param.astype(F32), axis=0)
    lower = jnp.cumsum(p_soft, axis=0) - p_soft[0:1]

    x = jnp.concatenate([x_prompt.reshape(n_prompt, d), x_sample.reshape(bs * ts, d)], axis=0)
    zero_state = jnp.zeros((bp, heads_a, HEAD_A, HEAD_A), F32)
    zero_hist = jnp.zeros((bp, HIST_ROWS, channels_b), F32)
    outs = {name: [] for name in ("hp", "cp", "kp", "vp", "hs", "cs", "ks", "vs")}

    for l in range(depth):
        h = rmsnorm_rows(x, g_mix[l], BF16)
        proj = gemm(h, w_in, l)

        o_a, s_p = hgrn_mixer(proj, lower[l], g_hgrn[l], zero_state, batch=bp, seq=tp,
                              chunk=_pick(tp, (128, 64, 32, 16, 8)), row0=0, heads=heads_a, total_rows=m_rows,
                              prev_out=jnp.zeros((m_rows, a_cols), F32))
        o_a, s_s = hgrn_mixer(proj, lower[l], g_hgrn[l], state_hgrn[l], batch=bs, seq=ts,
                              chunk=_pick(ts, (128, 64, 32, 16, 8)), row0=n_prompt, heads=heads_a, total_rows=m_rows,
                              prev_out=o_a)

        hist_s = jnp.pad(state_conv[l], ((0, 0), (HIST_ROWS - (conv_w - 1), 0), (0, 0)))
        conv, c_p = conv_mixer(proj, zero_hist, w_dw[l], b_dw[l], batch=bp, seq=tp,
                               tb=_pick(tp, (128, 64, 32, 16, 8)), row0=0, lin_col=lin_col, gate_col=gate_b_col,
                               total_rows=m_rows, prev_out=jnp.zeros((m_rows, channels_b), F32))
        conv, c_s = conv_mixer(proj, hist_s, w_dw[l], b_dw[l], batch=bs, seq=ts,
                               tb=_pick(ts, (128, 64, 32, 16, 8)), row0=n_prompt, lin_col=lin_col,
                               gate_col=gate_b_col, total_rows=m_rows, prev_out=conv)
        o_b = layernorm_silu_rows(conv, ln_g[l], ln_b[l], BF16)

        o_c, k_p, v_p = attn_prompt(proj, sinks[l], g_qn[l], g_kn[l], jnp.zeros((m_rows, q_heads * HEAD_C), F32),
                                    batch=bp, seq=tp, q_col=q_col, k_col=k_col, q_heads=q_heads, kv_heads=kv_heads)
        o_c, k_s, v_s = attn_sample(proj, cache_k[l].reshape(bs, WINDOW, kv_heads * HEAD_C),
                                    cache_v[l].reshape(bs, WINDOW, kv_heads * HEAD_C), sinks[l], g_qn[l], g_kn[l],
                                    o_c, batch=bs, seq=ts, start=past_len, row0=n_prompt, q_col=q_col, k_col=k_col,
                                    q_heads=q_heads, kv_heads=kv_heads)

        mixed = gated_merge(o_a, o_b, o_c, w_out_a, w_out_b, w_out_c, proj, l, merge_gate_col, d)
        x = gemm(mixed, w_o, l, residual=x)

        j = l // 2
        if l % 2 == 0:
            x = dense_ffn(x, g_ffn[l], w1_dense, w3_dense, w2_dense, j)
        else:
            x = moe_ffn(x, g_ffn[l], w_router[j], w1_moe, w3_moe, w2_moe, j)

        outs["hp"].append(s_p)
        outs["cp"].append(c_p)
        outs["kp"].append(k_p.reshape(bp, WINDOW, kv_heads, HEAD_C))
        outs["vp"].append(v_p.reshape(bp, WINDOW, kv_heads, HEAD_C))
        outs["hs"].append(s_s)
        outs["cs"].append(c_s)
        outs["ks"].append(k_s.reshape(bs, WINDOW, kv_heads, HEAD_C))
        outs["vs"].append(v_s.reshape(bs, WINDOW, kv_heads, HEAD_C))

    y_prompt = x[:n_prompt].reshape(bp, tp, d)
    y_sample = x[n_prompt:].reshape(bs, ts, d)
    return (y_prompt, y_sample,
            jnp.stack(outs["hp"]), jnp.stack(outs["cp"]), jnp.stack(outs["kp"]), jnp.stack(outs["vp"]),
            jnp.stack(outs["hs"]), jnp.stack(outs["cs"]), jnp.stack(outs["ks"]), jnp.stack(outs["vs"]))
```
